```python
import math
import jax, jax.numpy as jnp
from jax import lax
import numpy as np

D_MODEL = 1024
BATCH = 4
SEQ = 8192
DEPTH = 1

HEAD_DIM = 64
SB_HEADS = 6
MOBA_HEADS = 6
MEM_HEADS = 4
MEM_LEN = 256
SB_W = SB_HEADS * HEAD_DIM
MOBA_W = MOBA_HEADS * HEAD_DIM
MEM_W = MEM_HEADS * HEAD_DIM
SB_QBLOCK = 128
MOBA_BLOCK = 256
MOBA_TOPK = 3
MOBA_QCHUNK = 32
ROPE_THETA = 10000.0
D_FF = -(-8 * D_MODEL // (3 * 256)) * 256
N_BRANCH = 3
EPS = 1e-6
IN_SPLIT_SIZES = (SB_W, SB_W, SB_W, MOBA_W, MOBA_W, MOBA_W, MEM_W, D_MODEL, D_MODEL, D_MODEL)
IN_WIDTH = sum(IN_SPLIT_SIZES)

kernel_name = "hybrid_stickbreak_moba_memory_swiglu"


def rms_norm(x, g):
    xf = x.astype(jnp.float32)
    y = xf * lax.rsqrt(jnp.mean(xf * xf, axis=-1, keepdims=True) + EPS)
    return (y * g.astype(jnp.float32)).astype(x.dtype)


def head_rms(t, g):
    return t * lax.rsqrt(jnp.mean(t * t, axis=-1, keepdims=True) + EPS) * g.astype(jnp.float32)


def split_heads(t, n):
    b, s, _ = t.shape
    return t.reshape(b, s, n, HEAD_DIM).transpose(0, 2, 1, 3).astype(jnp.float32)


def merge_heads(t):
    b, h, s, d = t.shape
    return t.transpose(0, 2, 1, 3).reshape(b, s, h * d)


def rope(t, pos):
    half = HEAD_DIM // 2
    inv_freq = ROPE_THETA ** (-jnp.arange(half, dtype=jnp.float32) * 2.0 / HEAD_DIM)
    ang = pos[:, None] * inv_freq[None, :]
    cos, sin = jnp.cos(ang), jnp.sin(ang)
    t1, t2 = t[..., :half], t[..., half:]
    return jnp.concatenate([t1 * cos - t2 * sin, t2 * cos + t1 * sin], axis=-1)


def stick_breaking_attention(q, k, v):
    b, h, s, d = q.shape
    nqb = s // SB_QBLOCK
    scale = 1.0 / math.sqrt(d)
    qb = q.reshape(b, h, nqb, SB_QBLOCK, d).transpose(2, 0, 1, 3, 4)
    kpos = jnp.arange(s)

    def block(args):
        qi, i = args
        qpos = i * SB_QBLOCK + jnp.arange(SB_QBLOCK)
        z = jnp.einsum('bhqd,bhkd->bhqk', qi, k) * scale
        mask = kpos[None, :] < qpos[:, None]
        log_beta = jax.nn.log_sigmoid(z)
        log_1m = jnp.where(mask, jax.nn.log_sigmoid(-z), 0.0)
        tail = lax.cumsum(log_1m, axis=3, reverse=True) - log_1m
        a = jnp.where(mask, jnp.exp(log_beta + tail), 0.0)
        return jnp.einsum('bhqk,bhkd->bhqd', a, v)

    out = lax.map(block, (qb, jnp.arange(nqb)))
    return out.transpose(1, 2, 0, 3, 4).reshape(b, h, s, d)


def moba_attention(q, k, v):
    b, h, s, d = q.shape
    scale = 1.0 / math.sqrt(d)
    sp = -(-s // MOBA_BLOCK) * MOBA_BLOCK
    pad = ((0, 0), (0, 0), (0, sp - s), (0, 0))
    q, k, v = jnp.pad(q, pad), jnp.pad(k, pad), jnp.pad(v, pad)
    nb = sp // MOBA_BLOCK
    kb = k.reshape(b, h, nb, MOBA_BLOCK, d)
    vb = v.reshape(b, h, nb, MOBA_BLOCK, d)
    kmean = jnp.mean(kb, axis=3)
    pos = jnp.arange(sp)
    qblk = pos // MOBA_BLOCK
    gate = jnp.einsum('bhtd,bhnd->bhtn', q, kmean)
    past = jnp.arange(nb)[None, :] < qblk[:, None]
    gate = jnp.where(past, gate, -jnp.inf)
    topk = min(MOBA_TOPK, nb)
    _, sel = lax.top_k(gate, topk)
    sel_ok = jnp.arange(topk)[None, :] < qblk[:, None]

    nc = sp // MOBA_QCHUNK
    qc = q.reshape(b, h, nc, MOBA_QCHUNK, d).transpose(2, 0, 1, 3, 4)
    selc = sel.reshape(b, h, nc, MOBA_QCHUNK, topk).transpose(2, 0, 1, 3, 4)
    okc = sel_ok.reshape(nc, MOBA_QCHUNK, topk)
    gather = jax.vmap(jax.vmap(lambda blocks, idx: blocks[idx]))

    def chunk(args):
        qi, si, oki, c = args
        qpos = c * MOBA_QCHUNK + jnp.arange(MOBA_QCHUNK)
        own = (c * MOBA_QCHUNK) // MOBA_BLOCK
        k_own = lax.dynamic_index_in_dim(kb, own, axis=2, keepdims=False)
        v_own = lax.dynamic_index_in_dim(vb, own, axis=2, keepdims=False)
        k_sel = gather(kb, si)
        v_sel = gather(vb, si)
        s_sel = jnp.einsum('bhqd,bhqnkd->bhqnk', qi, k_sel) * scale
        s_sel = jnp.where(oki[None, None, :, :, None], s_sel, -jnp.inf)
        s_sel = s_sel.reshape(b, h, MOBA_QCHUNK, topk * MOBA_BLOCK)
        s_own = jnp.einsum('bhqd,bhkd->bhqk', qi, k_own) * scale
        kpos_own = own * MOBA_BLOCK + jnp.arange(MOBA_BLOCK)
        s_own = jnp.where(kpos_own[None, :] <= qpos[:, None], s_own, -jnp.inf)
        p = jax.nn.softmax(jnp.concatenate([s_sel, s_own], axis=-1), axis=-1)
        p_sel = p[..., :topk * MOBA_BLOCK].reshape(b, h, MOBA_QCHUNK, topk, MOBA_BLOCK)
        p_own = p[..., topk * MOBA_BLOCK:]
        return (jnp.einsum('bhqnk,bhqnkd->bhqd', p_sel, v_sel)
                + jnp.einsum('bhqk,bhkd->bhqd', p_own, v_own))

    out = lax.map(chunk, (qc, selc, okc, jnp.arange(nc)))
    out = out.transpose(1, 2, 0, 3, 4).reshape(b, h, sp, d)
    return out[:, :, :s]


def memory_cross_attention(q, k, v):
    scale = 1.0 / math.sqrt(q.shape[-1])
    p = jax.nn.softmax(jnp.einsum('bhsd,bhmd->bhsm', q, k) * scale, axis=-1)
    return jnp.einsum('bhsm,bhmd->bhsd', p, v)


def hybrid_layer(x, mem, mix_norm_g, mem_norm_g, ffn_norm_g, w_in, w_mem_kv,
                 moba_q_norm_g, moba_k_norm_g, mem_q_norm_g, mem_k_norm_g,
                 w_up_sb, w_up_moba, w_up_mem, w_out, w_ffn_in, w_ffn_down):
    s = x.shape[1]
    pos = jnp.arange(s, dtype=jnp.float32)
    h = rms_norm(x, mix_norm_g)
    proj = h @ w_in
    offsets = list(np.cumsum(IN_SPLIT_SIZES)[:-1].tolist())
    (sb_q, sb_k, sb_v, mo_q, mo_k, mo_v, me_q, g_sb, g_mo, g_me) = jnp.split(proj, offsets, axis=-1)

    o_sb = stick_breaking_attention(split_heads(sb_q, SB_HEADS), split_heads(sb_k, SB_HEADS),
                                    split_heads(sb_v, SB_HEADS))
    mq = rope(head_rms(split_heads(mo_q, MOBA_HEADS), moba_q_norm_g), pos)
    mk = rope(head_rms(split_heads(mo_k, MOBA_HEADS), moba_k_norm_g), pos)
    o_mo = moba_attention(mq, mk, split_heads(mo_v, MOBA_HEADS))
    mem_h = rms_norm(mem, mem_norm_g)
    mem_k, mem_v = jnp.split(mem_h @ w_mem_kv, 2, axis=-1)
    cq = head_rms(split_heads(me_q, MEM_HEADS), mem_q_norm_g)
    ck = head_rms(split_heads(mem_k, MEM_HEADS), mem_k_norm_g)
    o_me = memory_cross_attention(cq, ck, split_heads(mem_v, MEM_HEADS))

    dt = x.dtype
    up_sb = (merge_heads(o_sb).astype(dt) @ w_up_sb).astype(jnp.float32)
    up_mo = (merge_heads(o_mo).astype(dt) @ w_up_moba).astype(jnp.float32)
    up_me = (merge_heads(o_me).astype(dt) @ w_up_mem).astype(jnp.float32)
    mix = (jax.nn.sigmoid(g_sb.astype(jnp.float32)) * up_sb
           + jax.nn.sigmoid(g_mo.astype(jnp.float32)) * up_mo
           + jax.nn.sigmoid(g_me.astype(jnp.float32)) * up_me)
    x = x + (mix.astype(dt) @ w_out).astype(dt)

    h2 = rms_norm(x, ffn_norm_g)
    gate, up = jnp.split(h2 @ w_ffn_in, 2, axis=-1)
    ff = (jax.nn.silu(gate.astype(jnp.float32)) * up.astype(jnp.float32)).astype(dt)
    return x + (ff @ w_ffn_down).astype(dt)


def setup_inputs(seed: int = 0) -> dict:
    key = jax.random.key(seed)
    ks = jax.random.split(key, 20)
    f32 = jnp.float32

    def w(k, shape, fan_in):
        return jax.random.normal(k, shape, f32) * (fan_in ** -0.5)

    def gain(k, shape):
        return 1.0 + 0.02 * jax.random.normal(k, shape, f32)

    L = DEPTH
    return {
        "x": jax.random.normal(ks[0], (BATCH, SEQ, D_MODEL), f32),
        "mem": jax.random.normal(ks[1], (BATCH, MEM_LEN, D_MODEL), f32),
        "mix_norm_g": gain(ks[2], (L, D_MODEL)),
        "mem_norm_g": gain(ks[3], (L, D_MODEL)),
        "ffn_norm_g": gain(ks[4], (L, D_MODEL)),
        "w_in": w(ks[5], (L, D_MODEL, IN_WIDTH), D_MODEL),
        "w_mem_kv": w(ks[6], (L, D_MODEL, 2 * MEM_W), D_MODEL),
        "moba_q_norm_g": gain(ks[7], (L, HEAD_DIM)),
        "moba_k_norm_g": gain(ks[8], (L, HEAD_DIM)),
        "mem_q_norm_g": gain(ks[9], (L, HEAD_DIM)),
        "mem_k_norm_g": gain(ks[10], (L, HEAD_DIM)),
        "w_up_sb": w(ks[11], (L, SB_W, D_MODEL), SB_W),
        "w_up_moba": w(ks[12], (L, MOBA_W, D_MODEL), MOBA_W),
        "w_up_mem": w(ks[13], (L, MEM_W, D_MODEL), MEM_W),
        "w_out": w(ks[14], (L, D_MODEL, D_MODEL), D_MODEL),
        "w_ffn_in": w(ks[15], (L, D_MODEL, 2 * D_FF), D_MODEL),
        "w_ffn_down": w(ks[16], (L, D_FF, D_MODEL), D_FF),
    }


def reference(x, mem, mix_norm_g, mem_norm_g, ffn_norm_g, w_in, w_mem_kv,
              moba_q_norm_g, moba_k_norm_g, mem_q_norm_g, mem_k_norm_g,
              w_up_sb, w_up_moba, w_up_mem, w_out, w_ffn_in, w_ffn_down):
    for l in range(DEPTH):
        x = hybrid_layer(x, mem, mix_norm_g[l], mem_norm_g[l], ffn_norm_g[l], w_in[l], w_mem_kv[l],
                         moba_q_norm_g[l], moba_k_norm_g[l], mem_q_norm_g[l], mem_k_norm_g[l],
                         w_up_sb[l], w_up_moba[l], w_up_mem[l], w_out[l], w_ffn_in[l], w_ffn_down[l])
    return x
```

```python
import functools
import math

import jax
import jax.numpy as jnp
from jax import lax
from jax.experimental import pallas as pl
from jax.experimental.pallas import tpu as pltpu

F32 = jnp.float32
BF16 = jnp.bfloat16

HEAD_DIM = 64
SB_HEADS = 6
MOBA_HEADS = 6
MEM_HEADS = 4
SB_W = SB_HEADS * HEAD_DIM
MOBA_W = MOBA_HEADS * HEAD_DIM
MEM_W = MEM_HEADS * HEAD_DIM
QKV_W = 3 * SB_W + 3 * MOBA_W + MEM_W
MOBA_BLOCK = 256
MOBA_TOPK = 3
ROPE_THETA = 10000.0
EPS = 1e-6
QK_SCALE = 1.0 / math.sqrt(HEAD_DIM)

LANES = 128
NEG_BIG = -1e30

SB_DONE_LOG = -50.0

INPROJ_ROWS = 512
SB_ROWS = 256
SB_SUB = 128
SB_WIN = 256
MEMATTN_ROWS = 1024
TAIL_ROWS = 256
VMEM_LIMIT = 56 * 1024 * 1024


def _dot(a, b):
    return jnp.dot(a, b, preferred_element_type=F32)


def _dot_nt(a, b):
    return lax.dot_general(a, b, (((1,), (1,)), ((), ())), preferred_element_type=F32)


def _split_bf16(t):
    hi = t.astype(BF16)
    lo = (t - hi.astype(F32)).astype(BF16)
    return hi, lo


def _rms_rows(x, g):
    ms = jnp.mean(x * x, axis=-1, keepdims=True)
    return x * lax.rsqrt(ms + EPS) * g


def _head_rms(t, bd, g):
    hi, lo = _split_bf16(t * t)
    ms = _dot(hi, bd) + _dot(lo, bd)
    return t * lax.rsqrt(ms + EPS) * g


def _rotate_half_pairs(t):
    lane = lax.broadcasted_iota(jnp.int32, t.shape, 1)
    first_half = (lane & (HEAD_DIM - 1)) < (HEAD_DIM // 2)
    return jnp.where(first_half, pltpu.roll(t, LANES - HEAD_DIM // 2, 1), pltpu.roll(t, HEAD_DIM // 2, 1))


def _rope(t, cos, sin_signed):
    outs = []
    for c in range(t.shape[1] // LANES):
        tc = t[:, c * LANES:(c + 1) * LANES]
        outs.append(tc * cos + _rotate_half_pairs(tc) * sin_signed)
    return jnp.concatenate(outs, axis=1)


def _memkv_kernel(mem_ref, g_ref, w_ref, bd_ref, gk_ref, ck_ref, cv_ref):
    h = _rms_rows(mem_ref[0], g_ref[...]).astype(BF16)
    kv = _dot(h, w_ref[...])
    k = kv[:, :MEM_W]
    ck_ref[0] = _head_rms(k, bd_ref[...], gk_ref[...]).astype(BF16)
    cv_ref[0] = kv[:, MEM_W:].astype(BF16)


def _memkv(mem, g, w, bd, gk):
    b, m, d = mem.shape
    return pl.pallas_call(
        _memkv_kernel,
        grid=(b,),
        in_specs=[
            pl.BlockSpec((1, m, d), lambda i: (i, 0, 0)),
            pl.BlockSpec((1, d), lambda i: (0, 0)),
            pl.BlockSpec((d, 2 * MEM_W), lambda i: (0, 0)),
            pl.BlockSpec((MEM_W, MEM_W), lambda i: (0, 0)),
            pl.BlockSpec((1, MEM_W), lambda i: (0, 0)),
        ],
        out_specs=[
            pl.BlockSpec((1, m, MEM_W), lambda i: (i, 0, 0)),
            pl.BlockSpec((1, m, MEM_W), lambda i: (i, 0, 0)),
        ],
        out_shape=[jax.ShapeDtypeStruct((b, m, MEM_W), BF16)] * 2,
        compiler_params=pltpu.CompilerParams(dimension_semantics=("arbitrary",), vmem_limit_bytes=VMEM_LIMIT),
        name="memkv",
    )(mem, g, w, bd, gk)


def _inproj_kernel(x_ref, g_ref, w_ref, cos_ref, sin_ref, bd_ref, gq_ref, gk_ref, gmq_ref,
                   sbq_ref, sbk_ref, sbv_ref, moq_ref, mok_ref, mov_ref, meq_ref, kmean_ref):
    h = _rms_rows(x_ref[...], g_ref[...]).astype(BF16)
    proj = _dot(h, w_ref[...])
    o = 0
    sbq_ref[...] = (proj[:, o:o + SB_W] * QK_SCALE).astype(BF16); o += SB_W
    sbk_ref[...] = proj[:, o:o + SB_W].astype(BF16); o += SB_W
    sbv_ref[...] = proj[:, o:o + SB_W].astype(BF16); o += SB_W
    bd = bd_ref[...]
    cos = cos_ref[...]
    sin = sin_ref[...]
    mq = _rope(_head_rms(proj[:, o:o + MOBA_W], bd, gq_ref[...]), cos, sin); o += MOBA_W
    moq_ref[...] = (mq * QK_SCALE).astype(BF16)
    mk = _rope(_head_rms(proj[:, o:o + MOBA_W], bd, gk_ref[...]), cos, sin); o += MOBA_W
    mok_ref[...] = mk.astype(BF16)
    for blk in range(INPROJ_ROWS // MOBA_BLOCK):
        kmean_ref[0, blk:blk + 1, :] = jnp.mean(mk[blk * MOBA_BLOCK:(blk + 1) * MOBA_BLOCK], axis=0, keepdims=True)
    mov_ref[...] = proj[:, o:o + MOBA_W].astype(BF16); o += MOBA_W
    cq = _head_rms(proj[:, o:o + MEM_W], bd[:MEM_W, :MEM_W], gmq_ref[...])
    meq_ref[...] = (cq * QK_SCALE).astype(BF16)


def _inproj(x2, g, w, cos, sin, bd, gq, gk, gmq, seq):
    n, d = x2.shape
    tm = INPROJ_ROWS
    steps = n // tm
    per_seq = seq // tm
    blocks_per_step = tm // MOBA_BLOCK
    const = lambda i: (0, 0)
    rows = lambda i: (i, 0)
    out_shapes = [jax.ShapeDtypeStruct((n, SB_W), BF16)] * 3 + [jax.ShapeDtypeStruct((n, MOBA_W), BF16)] * 3 + [
        jax.ShapeDtypeStruct((n, MEM_W), BF16),
        jax.ShapeDtypeStruct((steps, blocks_per_step, MOBA_W), F32),
    ]
    out_specs = [pl.BlockSpec((tm, SB_W), rows)] * 3 + [pl.BlockSpec((tm, MOBA_W), rows)] * 3 + [
        pl.BlockSpec((tm, MEM_W), rows),
        pl.BlockSpec((1, blocks_per_step, MOBA_W), lambda i: (i, 0, 0)),
    ]
    return pl.pallas_call(
        _inproj_kernel,
        grid=(steps,),
        in_specs=[
            pl.BlockSpec((tm, d), rows),
            pl.BlockSpec((1, d), const),
            pl.BlockSpec((d, QKV_W), const),
            pl.BlockSpec((tm, LANES), lambda i: (i % per_seq, 0)),
            pl.BlockSpec((tm, LANES), lambda i: (i % per_seq, 0)),
            pl.BlockSpec((MOBA_W, MOBA_W), const),
            pl.BlockSpec((1, MOBA_W), const),
            pl.BlockSpec((1, MOBA_W), const),
            pl.BlockSpec((1, MEM_W), const),
        ],
        out_specs=out_specs,
        out_shape=out_shapes,
        compiler_params=pltpu.CompilerParams(dimension_semantics=("arbitrary",), vmem_limit_bytes=VMEM_LIMIT),
        name="inproj",
    )(x2, g, w, cos, sin, bd, gq, gk, gmq)


def _sb_kernel(q_ref, k_ref, v_ref, o_ref):
    i = pl.program_id(1)
    lane = lax.broadcasted_iota(jnp.int32, (SB_SUB, LANES), 1)
    low = lane < HEAD_DIM
    kiota = lax.broadcasted_iota(jnp.int32, (SB_SUB, SB_WIN), 1)
    riota = lax.broadcasted_iota(jnp.int32, (SB_SUB, 1), 0)
    ur = lax.broadcasted_iota(jnp.int32, (SB_WIN, SB_WIN), 0)
    uc = lax.broadcasted_iota(jnp.int32, (SB_WIN, SB_WIN), 1)
    later = (ur > uc).astype(BF16)
    n_pairs = SB_W // LANES

    def window(qm, pair, start, limit, carry):
        kt = k_ref[0, pl.ds(start, SB_WIN), pair * LANES:(pair + 1) * LANES]
        vt = v_ref[0, pl.ds(start, SB_WIN), pair * LANES:(pair + 1) * LANES]
        z = _dot_nt(qm, kt)
        valid = (start + kiota) < limit
        log_beta = jnp.minimum(z, 0.0) - jnp.log1p(jnp.exp(-jnp.abs(z)))
        log_1m = jnp.where(valid, log_beta - z, 0.0)
        hi, lo = _split_bf16(log_1m)
        tail = _dot(hi, later) + _dot(lo, later)
        a = jnp.where(valid, jnp.exp(log_beta + tail + carry), 0.0)
        return _dot(a.astype(BF16), vt), carry + jnp.sum(log_1m, axis=1, keepdims=True)

    for sub in range(SB_ROWS // SB_SUB):
        t0 = i * SB_ROWS + sub * SB_SUB
        qpos = t0 + riota
        qms = []
        for pair in range(n_pairs):
            q2 = q_ref[0, sub * SB_SUB:(sub + 1) * SB_SUB, pair * LANES:(pair + 1) * LANES]
            zero = jnp.zeros_like(q2)
            qms.append((jnp.where(low, q2, zero), jnp.where(low, zero, q2)))

        start0 = pl.multiple_of(jnp.maximum(t0 - (SB_WIN - SB_SUB), 0), SB_SUB)
        accs, carries = [], []
        for pair in range(n_pairs):
            for qm in qms[pair]:
                acc, carry = window(qm, pair, start0, qpos, jnp.zeros((SB_SUB, 1), F32))
                accs.append(acc)
                carries.append(carry)

        def unfinished(cs):
            worst = cs[0]
            for c in cs[1:]:
                worst = jnp.maximum(worst, c)
            return (jnp.max(worst) > SB_DONE_LOG).astype(jnp.int32)

        def cond(state):
            end, more = state[0], state[1]
            return jnp.logical_and(end > 0, more > 0)

        def body(state):
            end = state[0]
            accs_c = state[2:2 + 2 * n_pairs]
            cs_c = state[2 + 2 * n_pairs:]
            start = pl.multiple_of(jnp.maximum(end - SB_WIN, 0), SB_SUB)
            new_accs, new_cs = [], []
            for pair in range(n_pairs):
                for h, qm in enumerate(qms[pair]):
                    idx = 2 * pair + h
                    acc, carry = window(qm, pair, start, end, cs_c[idx])
                    new_accs.append(accs_c[idx] + acc)
                    new_cs.append(carry)
            return (start, unfinished(new_cs), *new_accs, *new_cs)

        state = lax.while_loop(cond, body, (start0, unfinished(carries), *accs, *carries))
        accs = state[2:2 + 2 * n_pairs]
        for pair in range(n_pairs):
            out = jnp.where(low, accs[2 * pair], accs[2 * pair + 1])
            o_ref[0, sub * SB_SUB:(sub + 1) * SB_SUB, pair * LANES:(pair + 1) * LANES] = out.astype(BF16)


def _sb_attention(q, k, v):
    b, s, w = q.shape
    return pl.pallas_call(
        _sb_kernel,
        grid=(b, s // SB_ROWS),
        in_specs=[
            pl.BlockSpec((1, SB_ROWS, w), lambda bi, i: (bi, i, 0)),
            pl.BlockSpec((1, s, w), lambda bi, i: (bi, 0, 0)),
            pl.BlockSpec((1, s, w), lambda bi, i: (bi, 0, 0)),
        ],
        out_specs=pl.BlockSpec((1, SB_ROWS, w), lambda bi, i: (bi, i, 0)),
        out_shape=jax.ShapeDtypeStruct((b, s, w), BF16),
        compiler_params=pltpu.CompilerParams(dimension_semantics=("arbitrary", "arbitrary"),
                                             vmem_limit_bytes=VMEM_LIMIT),
        name="sb_attention",
    )(q, k, v)


def _moba_kernel(q_ref, k_ref, v_ref, km_ref, o_ref):
    i = pl.program_id(2)
    nb = km_ref.shape[1]
    tq = MOBA_BLOCK
    lane = lax.broadcasted_iota(jnp.int32, (tq, LANES), 1)
    low = lane < HEAD_DIM
    q2 = q_ref[0]
    zero = jnp.zeros_like(q2)
    qms = (jnp.where(low, q2, zero), jnp.where(low, zero, q2))

    km_hi, km_lo = _split_bf16(km_ref[0])
    blk = lax.broadcasted_iota(jnp.int32, (tq, nb), 1)
    past = blk < i

    def select(qm):
        gate = _dot_nt(qm, km_hi) + _dot_nt(qm, km_lo)
        gate = jnp.where(past, gate, -jnp.inf)
        bits = jnp.zeros((tq, 1), jnp.int32)
        for _ in range(MOBA_TOPK):
            best = jnp.max(gate, axis=1, keepdims=True)
            idx = jnp.min(jnp.where(gate == best, blk, nb), axis=1, keepdims=True)
            bits = bits | jnp.where(best > -jnp.inf, jnp.left_shift(1, idx), 0)
            gate = jnp.where(blk == idx, -jnp.inf, gate)
        return bits

    bits = [select(qm) for qm in qms]

    own = pl.multiple_of(i * tq, tq)
    kd = k_ref[0, pl.ds(own, tq), :]
    vd = v_ref[0, pl.ds(own, tq), :]
    causal = lax.broadcasted_iota(jnp.int32, (tq, tq), 1) <= lax.broadcasted_iota(jnp.int32, (tq, tq), 0)
    state = []
    for qm in qms:
        s = jnp.where(causal, _dot_nt(qm, kd), NEG_BIG)
        m = jnp.max(s, axis=1, keepdims=True)
        p = jnp.exp(s - m)
        state += [m, jnp.sum(p, axis=1, keepdims=True), _dot(p.astype(BF16), vd)]

    def body(jj, st):
        st = list(st)
        for u in range(2):
            j = 2 * jj + u
            off = pl.multiple_of(j * tq, tq)
            kj = k_ref[0, pl.ds(off, tq), :]
            vj = v_ref[0, pl.ds(off, tq), :]
            for h, qm in enumerate(qms):
                m, l, acc = st[3 * h:3 * h + 3]
                picked = (jnp.right_shift(bits[h], j) & 1) > 0
                s = jnp.where(picked, _dot_nt(qm, kj), NEG_BIG)
                m_new = jnp.maximum(m, jnp.max(s, axis=1, keepdims=True))
                alpha = jnp.exp(m - m_new)
                p = jnp.exp(s - m_new)
                st[3 * h] = m_new
                st[3 * h + 1] = alpha * l + jnp.sum(p, axis=1, keepdims=True)
                st[3 * h + 2] = alpha * acc + _dot(p.astype(BF16), vj)
        return tuple(st)

    state = lax.fori_loop(0, (i + 1) // 2, body, tuple(state))
    out = jnp.where(low, state[2] / state[1], state[5] / state[4])
    o_ref[0] = out.astype(BF16)


def _moba_attention(q, k, v, kmean):
    b, s, w = q.shape
    nb = s // MOBA_BLOCK
    return pl.pallas_call(
        _moba_kernel,
        grid=(b, w // LANES, nb),
        in_specs=[
            pl.BlockSpec((1, MOBA_BLOCK, LANES), lambda bi, p, i: (bi, i, p)),
            pl.BlockSpec((1, s, LANES), lambda bi, p, i: (bi, 0, p)),
            pl.BlockSpec((1, s, LANES), lambda bi, p, i: (bi, 0, p)),
            pl.BlockSpec((1, nb, LANES), lambda bi, p, i: (bi, 0, p)),
        ],
        out_specs=pl.BlockSpec((1, MOBA_BLOCK, LANES), lambda bi, p, i: (bi, i, p)),
        out_shape=jax.ShapeDtypeStruct((b, s, w), BF16),
        compiler_params=pltpu.CompilerParams(dimension_semantics=("arbitrary",) * 3, vmem_limit_bytes=VMEM_LIMIT),
        name="moba_attention",
    )(q, k, v, kmean)


def _memattn_kernel(q_ref, ck_ref, cv_ref, o_ref):
    rows = q_ref.shape[1]
    lane = lax.broadcasted_iota(jnp.int32, (rows, LANES), 1)
    low = lane < HEAD_DIM
    for pair in range(MEM_W // LANES):
        cols = slice(pair * LANES, (pair + 1) * LANES)
        q2 = q_ref[0, :, cols]
        kp = ck_ref[0, :, cols]
        vp = cv_ref[0, :, cols]
        zero = jnp.zeros_like(q2)
        outs = []
        for qm in (jnp.where(low, q2, zero), jnp.where(low, zero, q2)):
            s = _dot_nt(qm, kp)
            p = jnp.exp(s - jnp.max(s, axis=1, keepdims=True))
            outs.append(_dot(p.astype(BF16), vp) / jnp.sum(p, axis=1, keepdims=True))
        o_ref[0, :, cols] = jnp.where(low, outs[0], outs[1]).astype(BF16)


def _mem_attention(q, ck, cv):
    b, s, w = q.shape
    m = ck.shape[1]
    tm = MEMATTN_ROWS
    return pl.pallas_call(
        _memattn_kernel,
        grid=(b, s // tm),
        in_specs=[
            pl.BlockSpec((1, tm, w), lambda bi, i: (bi, i, 0)),
            pl.BlockSpec((1, m, w), lambda bi, i: (bi, 0, 0)),
            pl.BlockSpec((1, m, w), lambda bi, i: (bi, 0, 0)),
        ],
        out_specs=pl.BlockSpec((1, tm, w), lambda bi, i: (bi, i, 0)),
        out_shape=jax.ShapeDtypeStruct((b, s, w), BF16),
        compiler_params=pltpu.CompilerParams(dimension_semantics=("arbitrary", "arbitrary"),
                                             vmem_limit_bytes=VMEM_LIMIT),
        name="mem_attention",
    )(q, ck, cv)


def _sigmoid(t):
    return 1.0 / (1.0 + jnp.exp(-t))


def _tail_kernel(x_ref, osb_ref, omo_ref, ome_ref, gmix_ref, gffn_ref, wg_ref, wsb_ref, wmo_ref, wme_ref,
                 wout_ref, wfi_ref, wfd_ref, out_ref):
    x = x_ref[...]
    d = x.shape[1]
    h = _rms_rows(x, gmix_ref[...]).astype(BF16)
    mix = None
    for n, (o_ref, wu_ref) in enumerate(((osb_ref, wsb_ref), (omo_ref, wmo_ref), (ome_ref, wme_ref))):
        gate = _dot(h, wg_ref[:, n * d:(n + 1) * d])
        term = _sigmoid(gate) * _dot(o_ref[...], wu_ref[...])
        mix = term if mix is None else mix + term
    x1 = x + _dot(mix.astype(BF16), wout_ref[...])
    h2 = _rms_rows(x1, gffn_ref[...]).astype(BF16)
    dff = wfd_ref.shape[0]
    gate = _dot(h2, wfi_ref[:, :dff])
    up = _dot(h2, wfi_ref[:, dff:])
    ff = (gate * _sigmoid(gate) * up).astype(BF16)
    out_ref[...] = x1 + _dot(ff, wfd_ref[...])


def _tail(x2, osb, omo, ome, gmix, gffn, wg, wsb, wmo, wme, wout, wfi, wfd):
    n, d = x2.shape
    tm = TAIL_ROWS
    rows = lambda i: (i, 0)

    def resident(arr):
        return pl.BlockSpec(arr.shape, lambda i: (0, 0), pipeline_mode=pl.Buffered(1))

    return pl.pallas_call(
        _tail_kernel,
        grid=(n // tm,),
        in_specs=[
            pl.BlockSpec((tm, d), rows),
            pl.BlockSpec((tm, SB_W), rows),
            pl.BlockSpec((tm, MOBA_W), rows),
            pl.BlockSpec((tm, MEM_W), rows),
            resident(gmix), resident(gffn), resident(wg), resident(wsb), resident(wmo), resident(wme),
            resident(wout), resident(wfi), resident(wfd),
        ],
        out_specs=pl.BlockSpec((tm, d), rows),
        out_shape=jax.ShapeDtypeStruct((n, d), F32),
        compiler_params=pltpu.CompilerParams(dimension_semantics=("arbitrary",), vmem_limit_bytes=VMEM_LIMIT),
        name="merge_out_ffn",
    )(x2, osb, omo, ome, gmix, gffn, wg, wsb, wmo, wme, wout, wfi, wfd)


def _rope_tables(seq):
    half = HEAD_DIM // 2
    inv_freq = ROPE_THETA ** (-jnp.arange(half, dtype=F32) * 2.0 / HEAD_DIM)
    ang = jnp.arange(seq, dtype=F32)[:, None] * inv_freq[None, :]
    cos, sin = jnp.cos(ang), jnp.sin(ang)
    reps = LANES // HEAD_DIM
    return jnp.tile(jnp.concatenate([cos, cos], axis=1), (1, reps)), jnp.tile(jnp.concatenate([-sin, sin], axis=1), (1, reps))


def _head_mean_matrix(width):
    r = jnp.arange(width) // HEAD_DIM
    return jnp.where(r[:, None] == r[None, :], 1.0 / HEAD_DIM, 0.0).astype(BF16)


def _layer(x, mem, mix_norm_g, mem_norm_g, ffn_norm_g, w_in, w_mem_kv, moba_q_norm_g, moba_k_norm_g,
           mem_q_norm_g, mem_k_norm_g, w_up_sb, w_up_moba, w_up_mem, w_out, w_ffn_in, w_ffn_down):
    b, s, d = x.shape
    x2 = x.reshape(b * s, d)
    cos, sin = _rope_tables(s)
    bd = _head_mean_matrix(MOBA_W)
    row = lambda g, reps=1: jnp.tile(g.astype(F32), reps)[None, :]

    ck, cv = _memkv(mem, row(mem_norm_g), w_mem_kv.astype(BF16), bd[:MEM_W, :MEM_W], row(mem_k_norm_g, MEM_HEADS))
    sbq, sbk, sbv, moq, mok, mov, meq, kmean = _inproj(
        x2, row(mix_norm_g), w_in[:, :QKV_W].astype(BF16), cos, sin, bd,
        row(moba_q_norm_g, MOBA_HEADS), row(moba_k_norm_g, MOBA_HEADS), row(mem_q_norm_g, MEM_HEADS), s)
    seq3 = lambda t: t.reshape(b, s, t.shape[-1])
    o_sb = _sb_attention(seq3(sbq), seq3(sbk), seq3(sbv))
    o_mo = _moba_attention(seq3(moq), seq3(mok), seq3(mov), kmean.reshape(b, s // MOBA_BLOCK, MOBA_W))
    o_me = _mem_attention(seq3(meq), ck, cv)
    flat = lambda t: t.reshape(b * s, t.shape[-1])
    out = _tail(x2, flat(o_sb), flat(o_mo), flat(o_me), row(mix_norm_g), row(ffn_norm_g),
                w_in[:, QKV_W:].astype(BF16), w_up_sb.astype(BF16), w_up_moba.astype(BF16), w_up_mem.astype(BF16),
                w_out.astype(BF16), w_ffn_in.astype(BF16), w_ffn_down.astype(BF16))
    return out.reshape(b, s, d)


def kernel(x, mem, mix_norm_g, mem_norm_g, ffn_norm_g, w_in, w_mem_kv, moba_q_norm_g, moba_k_norm_g, mem_q_norm_g, mem_k_norm_g, w_up_sb, w_up_moba, w_up_mem, w_out, w_ffn_in, w_ffn_down):
    for l in range(w_in.shape[0]):
        x = _layer(x, mem, mix_norm_g[l], mem_norm_g[l], ffn_norm_g[l], w_in[l], w_mem_kv[l],
                   moba_q_norm_g[l], moba_k_norm_g[l], mem_q_norm_g[l], mem_k_norm_g[l],
                   w_up_sb[l], w_up_moba[l], w_up_mem[l], w_out[l], w_ffn_in[l], w_ffn_down[l])
    return x
```

```python
import functools
import math

import jax
import jax.numpy as jnp
from jax import lax
from jax.experimental import pallas as pl
from jax.experimental.pallas import tpu as pltpu

F32 = jnp.float32
BF16 = jnp.bfloat16

HEAD_DIM = 64
SB_HEADS = 6
MOBA_HEADS = 6
MEM_HEADS = 4
SB_W = SB_HEADS * HEAD_DIM
MOBA_W = MOBA_HEADS * HEAD_DIM
MEM_W = MEM_HEADS * HEAD_DIM
QKV_W = 3 * SB_W + 3 * MOBA_W + MEM_W
MOBA_BLOCK = 256
MOBA_TOPK = 3
ROPE_THETA = 10000.0
EPS = 1e-6
QK_SCALE = 1.0 / math.sqrt(HEAD_DIM)

LANES = 128
BF16_SUBLANES = 16
VT_ROWS = HEAD_DIM + BF16_SUBLANES
NEG_BIG = -1e30

SB_DONE_LOG = -50.0

INPROJ_ROWS = 512
SB_ROWS = 256
SB_SUB = 128
SB_WIN = 256
MOBA_LOOKAHEAD = 2
MEMATTN_ROWS = 1024
TAIL_ROWS = 256
VMEM_LIMIT = 56 * 1024 * 1024


def _dot(a, b):
    return jnp.dot(a, b, preferred_element_type=F32)


def _dot_nt(a, b):
    return lax.dot_general(a, b, (((1,), (1,)), ((), ())), preferred_element_type=F32)


def _split_bf16(t):
    hi = t.astype(BF16)
    lo = (t - hi.astype(F32)).astype(BF16)
    return hi, lo


def _rms_rows(x, g):
    ms = jnp.mean(x * x, axis=-1, keepdims=True)
    return x * lax.rsqrt(ms + EPS) * g


def _head_rms(t, bd, g):
    hi, lo = _split_bf16(t * t)
    ms = _dot(hi, bd) + _dot(lo, bd)
    return t * lax.rsqrt(ms + EPS) * g


def _rotate_half_pairs(t):
    lane = lax.broadcasted_iota(jnp.int32, t.shape, 1)
    first_half = (lane & (HEAD_DIM - 1)) < (HEAD_DIM // 2)
    return jnp.where(first_half, pltpu.roll(t, LANES - HEAD_DIM // 2, 1), pltpu.roll(t, HEAD_DIM // 2, 1))


def _rope(t, cos, sin_signed):
    outs = []
    for c in range(t.shape[1] // LANES):
        tc = t[:, c * LANES:(c + 1) * LANES]
        outs.append(tc * cos + _rotate_half_pairs(tc) * sin_signed)
    return jnp.concatenate(outs, axis=1)


def _memkv_kernel(mem_ref, g_ref, w_ref, bd_ref, gk_ref, ck_ref, cv_ref):
    h = _rms_rows(mem_ref[0], g_ref[...]).astype(BF16)
    kv = _dot(h, w_ref[...])
    k = kv[:, :MEM_W]
    ck_ref[0] = _head_rms(k, bd_ref[...], gk_ref[...]).astype(BF16)
    cv_ref[0] = kv[:, MEM_W:].astype(BF16)


def _memkv(mem, g, w, bd, gk):
    b, m, d = mem.shape
    return pl.pallas_call(
        _memkv_kernel,
        grid=(b,),
        in_specs=[
            pl.BlockSpec((1, m, d), lambda i: (i, 0, 0)),
            pl.BlockSpec((1, d), lambda i: (0, 0)),
            pl.BlockSpec((d, 2 * MEM_W), lambda i: (0, 0)),
            pl.BlockSpec((MEM_W, MEM_W), lambda i: (0, 0)),
            pl.BlockSpec((1, MEM_W), lambda i: (0, 0)),
        ],
        out_specs=[
            pl.BlockSpec((1, m, MEM_W), lambda i: (i, 0, 0)),
            pl.BlockSpec((1, m, MEM_W), lambda i: (i, 0, 0)),
        ],
        out_shape=[jax.ShapeDtypeStruct((b, m, MEM_W), BF16)] * 2,
        compiler_params=pltpu.CompilerParams(dimension_semantics=("arbitrary",), vmem_limit_bytes=VMEM_LIMIT),
        name="memkv",
    )(mem, g, w, bd, gk)


def _inproj_kernel(x_ref, g_ref, w_ref, cos_ref, sin_ref, bd_ref, gq_ref, gk_ref, gmq_ref,
                   sbq_ref, sbk_ref, sbv_ref, moq_ref, mok_ref, movt_ref, meq_ref, kmean_ref):
    h = _rms_rows(x_ref[...], g_ref[...]).astype(BF16)
    proj = _dot(h, w_ref[...])
    o = 0
    sbq_ref[...] = (proj[:, o:o + SB_W] * QK_SCALE).astype(BF16); o += SB_W
    sbk_ref[...] = proj[:, o:o + SB_W].astype(BF16); o += SB_W
    sbv_ref[...] = proj[:, o:o + SB_W].astype(BF16); o += SB_W
    bd = bd_ref[...]
    cos = cos_ref[...]
    sin = sin_ref[...]
    mq = _rope(_head_rms(proj[:, o:o + MOBA_W], bd, gq_ref[...]), cos, sin); o += MOBA_W
    moq_ref[...] = (mq * QK_SCALE).astype(BF16)
    mk = _rope(_head_rms(proj[:, o:o + MOBA_W], bd, gk_ref[...]), cos, sin); o += MOBA_W
    mok_ref[...] = mk.astype(BF16)
    for blk in range(INPROJ_ROWS // MOBA_BLOCK):
        kmean_ref[0, blk:blk + 1, :] = jnp.mean(mk[blk * MOBA_BLOCK:(blk + 1) * MOBA_BLOCK], axis=0, keepdims=True)
    vt = proj[:, o:o + MOBA_W].T; o += MOBA_W
    for hd in range(MOBA_HEADS):
        movt_ref[0, hd, 0:HEAD_DIM, :] = vt[hd * HEAD_DIM:(hd + 1) * HEAD_DIM, :].astype(BF16)
        movt_ref[0, hd, HEAD_DIM:, :] = jnp.ones((VT_ROWS - HEAD_DIM, vt.shape[1]), BF16)
    cq = _head_rms(proj[:, o:o + MEM_W], bd[:MEM_W, :MEM_W], gmq_ref[...])
    meq_ref[...] = (cq * QK_SCALE).astype(BF16)


def _inproj(x2, g, w, cos, sin, bd, gq, gk, gmq, seq):
    n, d = x2.shape
    tm = INPROJ_ROWS
    steps = n // tm
    per_seq = seq // tm
    blocks_per_step = tm // MOBA_BLOCK
    const = lambda i: (0, 0)
    rows = lambda i: (i, 0)
    out_shapes = [jax.ShapeDtypeStruct((n, SB_W), BF16)] * 3 + [jax.ShapeDtypeStruct((n, MOBA_W), BF16)] * 2 + [
        jax.ShapeDtypeStruct((n // seq, MOBA_HEADS, VT_ROWS, seq), BF16),
        jax.ShapeDtypeStruct((n, MEM_W), BF16),
        jax.ShapeDtypeStruct((steps, blocks_per_step, MOBA_W), F32),
    ]
    out_specs = [pl.BlockSpec((tm, SB_W), rows)] * 3 + [pl.BlockSpec((tm, MOBA_W), rows)] * 2 + [
        pl.BlockSpec((1, MOBA_HEADS, VT_ROWS, tm), lambda i: (i // per_seq, 0, 0, i % per_seq)),
        pl.BlockSpec((tm, MEM_W), rows),
        pl.BlockSpec((1, blocks_per_step, MOBA_W), lambda i: (i, 0, 0)),
    ]
    return pl.pallas_call(
        _inproj_kernel,
        grid=(steps,),
        in_specs=[
            pl.BlockSpec((tm, d), rows),
            pl.BlockSpec((1, d), const),
            pl.BlockSpec((d, QKV_W), const),
            pl.BlockSpec((tm, LANES), lambda i: (i % per_seq, 0)),
            pl.BlockSpec((tm, LANES), lambda i: (i % per_seq, 0)),
            pl.BlockSpec((MOBA_W, MOBA_W), const),
            pl.BlockSpec((1, MOBA_W), const),
            pl.BlockSpec((1, MOBA_W), const),
            pl.BlockSpec((1, MEM_W), const),
        ],
        out_specs=out_specs,
        out_shape=out_shapes,
        compiler_params=pltpu.CompilerParams(dimension_semantics=("arbitrary",), vmem_limit_bytes=VMEM_LIMIT),
        name="inproj",
    )(x2, g, w, cos, sin, bd, gq, gk, gmq)


def _sb_kernel(q_ref, k_ref, v_ref, o_ref):
    i = pl.program_id(1)
    lane = lax.broadcasted_iota(jnp.int32, (SB_SUB, LANES), 1)
    low = lane < HEAD_DIM
    kiota = lax.broadcasted_iota(jnp.int32, (SB_SUB, SB_WIN), 1)
    riota = lax.broadcasted_iota(jnp.int32, (SB_SUB, 1), 0)
    ur = lax.broadcasted_iota(jnp.int32, (SB_WIN, SB_WIN), 0)
    uc = lax.broadcasted_iota(jnp.int32, (SB_WIN, SB_WIN), 1)
    later = (ur > uc).astype(BF16)
    n_pairs = SB_W // LANES

    def window(qm, pair, start, limit, carry):
        kt = k_ref[0, pl.ds(start, SB_WIN), pair * LANES:(pair + 1) * LANES]
        vt = v_ref[0, pl.ds(start, SB_WIN), pair * LANES:(pair + 1) * LANES]
        z = _dot_nt(qm, kt)
        valid = (start + kiota) < limit
        log_beta = jnp.minimum(z, 0.0) - jnp.log1p(jnp.exp(-jnp.abs(z)))
        log_1m = jnp.where(valid, log_beta - z, 0.0)
        hi, lo = _split_bf16(log_1m)
        tail = _dot(hi, later) + _dot(lo, later)
        a = jnp.where(valid, jnp.exp(log_beta + tail + carry), 0.0)
        return _dot(a.astype(BF16), vt), carry + jnp.sum(log_1m, axis=1, keepdims=True)

    for sub in range(SB_ROWS // SB_SUB):
        t0 = i * SB_ROWS + sub * SB_SUB
        qpos = t0 + riota
        qms = []
        for pair in range(n_pairs):
            q2 = q_ref[0, sub * SB_SUB:(sub + 1) * SB_SUB, pair * LANES:(pair + 1) * LANES]
            zero = jnp.zeros_like(q2)
            qms.append((jnp.where(low, q2, zero), jnp.where(low, zero, q2)))

        start0 = pl.multiple_of(jnp.maximum(t0 - (SB_WIN - SB_SUB), 0), SB_SUB)
        accs, carries = [], []
        for pair in range(n_pairs):
            for qm in qms[pair]:
                acc, carry = window(qm, pair, start0, qpos, jnp.zeros((SB_SUB, 1), F32))
                accs.append(acc)
                carries.append(carry)

        def unfinished(cs):
            worst = cs[0]
            for c in cs[1:]:
                worst = jnp.maximum(worst, c)
            return (jnp.max(worst) > SB_DONE_LOG).astype(jnp.int32)

        def cond(state):
            end, more = state[0], state[1]
            return jnp.logical_and(end > 0, more > 0)

        def body(state):
            end = state[0]
            accs_c = state[2:2 + 2 * n_pairs]
            cs_c = state[2 + 2 * n_pairs:]
            start = pl.multiple_of(jnp.maximum(end - SB_WIN, 0), SB_SUB)
            new_accs, new_cs = [], []
            for pair in range(n_pairs):
                for h, qm in enumerate(qms[pair]):
                    idx = 2 * pair + h
                    acc, carry = window(qm, pair, start, end, cs_c[idx])
                    new_accs.append(accs_c[idx] + acc)
                    new_cs.append(carry)
            return (start, unfinished(new_cs), *new_accs, *new_cs)

        state = lax.while_loop(cond, body, (start0, unfinished(carries), *accs, *carries))
        accs = state[2:2 + 2 * n_pairs]
        for pair in range(n_pairs):
            out = jnp.where(low, accs[2 * pair], accs[2 * pair + 1])
            o_ref[0, sub * SB_SUB:(sub + 1) * SB_SUB, pair * LANES:(pair + 1) * LANES] = out.astype(BF16)


def _sb_attention(q, k, v):
    b, s, w = q.shape
    return pl.pallas_call(
        _sb_kernel,
        grid=(b, s // SB_ROWS),
        in_specs=[
            pl.BlockSpec((1, SB_ROWS, w), lambda bi, i: (bi, i, 0)),
            pl.BlockSpec((1, s, w), lambda bi, i: (bi, 0, 0)),
            pl.BlockSpec((1, s, w), lambda bi, i: (bi, 0, 0)),
        ],
        out_specs=pl.BlockSpec((1, SB_ROWS, w), lambda bi, i: (bi, i, 0)),
        out_shape=jax.ShapeDtypeStruct((b, s, w), BF16),
        compiler_params=pltpu.CompilerParams(dimension_semantics=("arbitrary", "arbitrary"),
                                             vmem_limit_bytes=VMEM_LIMIT),
        name="sb_attention",
    )(q, k, v)


def _moba_kernel(q_ref, k_ref, vt_ref, km_ref, o_ref):
    i = pl.program_id(1)
    nb = km_ref.shape[1]
    tq = MOBA_BLOCK
    n_heads = vt_ref.shape[1]
    lane = lax.broadcasted_iota(jnp.int32, (tq, LANES), 1)
    low = lane < HEAD_DIM
    blk = lax.broadcasted_iota(jnp.int32, (nb, tq), 0)
    past = blk < i

    def cols(h):
        pair = h // 2
        return slice(pair * LANES, (pair + 1) * LANES)

    qms = []
    for pair in range(n_heads // 2):
        q2 = q_ref[0, :, pair * LANES:(pair + 1) * LANES]
        zero = jnp.zeros_like(q2)
        qms += [jnp.where(low, q2, zero), jnp.where(low, zero, q2)]

    gates = []
    for h in range(n_heads):
        km_hi, km_lo = _split_bf16(km_ref[0, :, cols(h)])
        gates.append(_dot_nt(km_hi, qms[h]) + _dot_nt(km_lo, qms[h]))

    def select(h):
        gate = jnp.where(past, gates[h], -jnp.inf)
        bits = jnp.zeros((1, tq), jnp.int32)
        for _ in range(MOBA_TOPK):
            best = jnp.max(gate, axis=0, keepdims=True)
            idx = jnp.min(jnp.where(gate == best, blk, nb), axis=0, keepdims=True)
            bits = bits | jnp.where(best > -jnp.inf, jnp.left_shift(1, idx), 0)
            gate = jnp.where(blk == idx, -jnp.inf, gate)
        return bits

    bits = [select(h) for h in range(n_heads)]

    own = pl.multiple_of(i * tq, tq)
    causal = lax.broadcasted_iota(jnp.int32, (tq, tq), 0) <= lax.broadcasted_iota(jnp.int32, (tq, tq), 1)
    state = []
    diag = [_dot_nt(k_ref[0, pl.ds(own, tq), cols(h)], qms[h]) for h in range(MOBA_LOOKAHEAD)]
    for h in range(n_heads):
        if h + MOBA_LOOKAHEAD < n_heads:
            g = h + MOBA_LOOKAHEAD
            diag.append(_dot_nt(k_ref[0, pl.ds(own, tq), cols(g)], qms[g]))
        s = jnp.where(causal, diag[h], NEG_BIG)
        m = jnp.max(s, axis=0, keepdims=True)
        p = jnp.exp(s - m).astype(BF16)
        state += [m, _dot(vt_ref[0, h, :, pl.ds(own, tq)], p)]

    def body(jj, st):
        st = list(st)
        offs = [pl.multiple_of((2 * jj + u) * tq, tq) for u in range(2)]
        def scores(h):
            return [_dot_nt(k_ref[0, pl.ds(offs[u], tq), cols(h)], qms[h]) for u in range(2)]

        queued = [scores(h) for h in range(MOBA_LOOKAHEAD)]
        for h in range(n_heads):
            if h + MOBA_LOOKAHEAD < n_heads:
                queued.append(scores(h + MOBA_LOOKAHEAD))
            raw = queued[h]
            m, acc = st[2 * h:2 * h + 2]
            ss = []
            for u in range(2):
                picked = (jnp.right_shift(bits[h], 2 * jj + u) & 1) > 0
                ss.append(jnp.where(picked, raw[u], NEG_BIG))
            m_new = jnp.maximum(m, jnp.max(jnp.maximum(ss[0], ss[1]), axis=0, keepdims=True))
            acc = jnp.exp(m - m_new) * acc
            for u in range(2):
                p = jnp.exp(ss[u] - m_new).astype(BF16)
                acc = acc + _dot(vt_ref[0, h, :, pl.ds(offs[u], tq)], p)
            st[2 * h] = m_new
            st[2 * h + 1] = acc
        return tuple(st)

    state = lax.fori_loop(0, (i + 1) // 2, body, tuple(state))
    for pair in range(n_heads // 2):
        outs = []
        for h in (2 * pair, 2 * pair + 1):
            acc = state[2 * h + 1]
            outs.append(acc[:HEAD_DIM] / acc[HEAD_DIM:HEAD_DIM + 1])
        o_ref[0, :, pair * LANES:(pair + 1) * LANES] = jnp.concatenate(outs, axis=0).T.astype(BF16)


def _moba_attention(q, k, vt, kmean):
    b, s, w = q.shape
    nb = s // MOBA_BLOCK
    return pl.pallas_call(
        _moba_kernel,
        grid=(b, nb),
        in_specs=[
            pl.BlockSpec((1, MOBA_BLOCK, w), lambda bi, i: (bi, i, 0)),
            pl.BlockSpec((1, s, w), lambda bi, i: (bi, 0, 0)),
            pl.BlockSpec((1, w // HEAD_DIM, VT_ROWS, s), lambda bi, i: (bi, 0, 0, 0)),
            pl.BlockSpec((1, nb, w), lambda bi, i: (bi, 0, 0)),
        ],
        out_specs=pl.BlockSpec((1, MOBA_BLOCK, w), lambda bi, i: (bi, i, 0)),
        out_shape=jax.ShapeDtypeStruct((b, s, w), BF16),
        compiler_params=pltpu.CompilerParams(dimension_semantics=("arbitrary",) * 2, vmem_limit_bytes=VMEM_LIMIT),
        name="moba_attention",
    )(q, k, vt, kmean)


def _memattn_kernel(q_ref, ck_ref, cv_ref, o_ref):
    rows = q_ref.shape[1]
    lane = lax.broadcasted_iota(jnp.int32, (rows, LANES), 1)
    low = lane < HEAD_DIM
    for pair in range(MEM_W // LANES):
        cols = slice(pair * LANES, (pair + 1) * LANES)
        q2 = q_ref[0, :, cols]
        kp = ck_ref[0, :, cols]
        vp = cv_ref[0, :, cols]
        zero = jnp.zeros_like(q2)
        outs = []
        for qm in (jnp.where(low, q2, zero), jnp.where(low, zero, q2)):
            s = _dot_nt(qm, kp)
            p = jnp.exp(s - jnp.max(s, axis=1, keepdims=True))
            outs.append(_dot(p.astype(BF16), vp) / jnp.sum(p, axis=1, keepdims=True))
        o_ref[0, :, cols] = jnp.where(low, outs[0], outs[1]).astype(BF16)


def _mem_attention(q, ck, cv):
    b, s, w = q.shape
    m = ck.shape[1]
    tm = MEMATTN_ROWS
    return pl.pallas_call(
        _memattn_kernel,
        grid=(b, s // tm),
        in_specs=[
            pl.BlockSpec((1, tm, w), lambda bi, i: (bi, i, 0)),
            pl.BlockSpec((1, m, w), lambda bi, i: (bi, 0, 0)),
            pl.BlockSpec((1, m, w), lambda bi, i: (bi, 0, 0)),
        ],
        out_specs=pl.BlockSpec((1, tm, w), lambda bi, i: (bi, i, 0)),
        out_shape=jax.ShapeDtypeStruct((b, s, w), BF16),
        compiler_params=pltpu.CompilerParams(dimension_semantics=("arbitrary", "arbitrary"),
                                             vmem_limit_bytes=VMEM_LIMIT),
        name="mem_attention",
    )(q, ck, cv)


def _sigmoid(t):
    return 1.0 / (1.0 + jnp.exp(-t))


def _tail_kernel(x_ref, osb_ref, omo_ref, ome_ref, gmix_ref, gffn_ref, wg_ref, wsb_ref, wmo_ref, wme_ref,
                 wout_ref, wfi_ref, wfd_ref, out_ref):
    x = x_ref[...]
    d = x.shape[1]
    h = _rms_rows(x, gmix_ref[...]).astype(BF16)
    mix = None
    for n, (o_ref, wu_ref) in enumerate(((osb_ref, wsb_ref), (omo_ref, wmo_ref), (ome_ref, wme_ref))):
        gate = _dot(h, wg_ref[:, n * d:(n + 1) * d])
        term = _sigmoid(gate) * _dot(o_ref[...], wu_ref[...])
        mix = term if mix is None else mix + term
    x1 = x + _dot(mix.astype(BF16), wout_ref[...])
    h2 = _rms_rows(x1, gffn_ref[...]).astype(BF16)
    dff = wfd_ref.shape[0]
    gate = _dot(h2, wfi_ref[:, :dff])
    up = _dot(h2, wfi_ref[:, dff:])
    ff = (gate * _sigmoid(gate) * up).astype(BF16)
    out_ref[...] = x1 + _dot(ff, wfd_ref[...])


def _tail(x2, osb, omo, ome, gmix, gffn, wg, wsb, wmo, wme, wout, wfi, wfd):
    n, d = x2.shape
    tm = TAIL_ROWS
    rows = lambda i: (i, 0)

    def resident(arr):
        return pl.BlockSpec(arr.shape, lambda i: (0, 0), pipeline_mode=pl.Buffered(1))

    return pl.pallas_call(
        _tail_kernel,
        grid=(n // tm,),
        in_specs=[
            pl.BlockSpec((tm, d), rows),
            pl.BlockSpec((tm, SB_W), rows),
            pl.BlockSpec((tm, MOBA_W), rows),
            pl.BlockSpec((tm, MEM_W), rows),
            resident(gmix), resident(gffn), resident(wg), resident(wsb), resident(wmo), resident(wme),
            resident(wout), resident(wfi), resident(wfd),
        ],
        out_specs=pl.BlockSpec((tm, d), rows),
        out_shape=jax.ShapeDtypeStruct((n, d), F32),
        compiler_params=pltpu.CompilerParams(dimension_semantics=("arbitrary",), vmem_limit_bytes=VMEM_LIMIT),
        name="merge_out_ffn",
    )(x2, osb, omo, ome, gmix, gffn, wg, wsb, wmo, wme, wout, wfi, wfd)


def _rope_tables(seq):
    half = HEAD_DIM // 2
    inv_freq = ROPE_THETA ** (-jnp.arange(half, dtype=F32) * 2.0 / HEAD_DIM)
    ang = jnp.arange(seq, dtype=F32)[:, None] * inv_freq[None, :]
    cos, sin = jnp.cos(ang), jnp.sin(ang)
    reps = LANES // HEAD_DIM
    return jnp.tile(jnp.concatenate([cos, cos], axis=1), (1, reps)), jnp.tile(jnp.concatenate([-sin, sin], axis=1), (1, reps))


def _head_mean_matrix(width):
    r = jnp.arange(width) // HEAD_DIM
    return jnp.where(r[:, None] == r[None, :], 1.0 / HEAD_DIM, 0.0).astype(BF16)


def _layer(x, mem, mix_norm_g, mem_norm_g, ffn_norm_g, w_in, w_mem_kv, moba_q_norm_g, moba_k_norm_g,
           mem_q_norm_g, mem_k_norm_g, w_up_sb, w_up_moba, w_up_mem, w_out, w_ffn_in, w_ffn_down):
    b, s, d = x.shape
    x2 = x.reshape(b * s, d)
    cos, sin = _rope_tables(s)
    bd = _head_mean_matrix(MOBA_W)
    row = lambda g, reps=1: jnp.tile(g.astype(F32), reps)[None, :]

    ck, cv = _memkv(mem, row(mem_norm_g), w_mem_kv.astype(BF16), bd[:MEM_W, :MEM_W], row(mem_k_norm_g, MEM_HEADS))
    sbq, sbk, sbv, moq, mok, movt, meq, kmean = _inproj(
        x2, row(mix_norm_g), w_in[:, :QKV_W].astype(BF16), cos, sin, bd,
        row(moba_q_norm_g, MOBA_HEADS), row(moba_k_norm_g, MOBA_HEADS), row(mem_q_norm_g, MEM_HEADS), s)
    seq3 = lambda t: t.reshape(b, s, t.shape[-1])
    o_sb = _sb_attention(seq3(sbq), seq3(sbk), seq3(sbv))
    o_mo = _moba_attention(seq3(moq), seq3(mok), movt, kmean.reshape(b, s // MOBA_BLOCK, MOBA_W))
    o_me = _mem_attention(seq3(meq), ck, cv)
    flat = lambda t: t.reshape(b * s, t.shape[-1])
    out = _tail(x2, flat(o_sb), flat(o_mo), flat(o_me), row(mix_norm_g), row(ffn_norm_g),
                w_in[:, QKV_W:].astype(BF16), w_up_sb.astype(BF16), w_up_moba.astype(BF16), w_up_mem.astype(BF16),
                w_out.astype(BF16), w_ffn_in.astype(BF16), w_ffn_down.astype(BF16))
    return out.reshape(b, s, d)


def kernel(x, mem, mix_norm_g, mem_norm_g, ffn_norm_g, w_in, w_mem_kv, moba_q_norm_g, moba_k_norm_g, mem_q_norm_g, mem_k_norm_g, w_up_sb, w_up_moba, w_up_mem, w_out, w_ffn_in, w_ffn_down):
    for l in range(w_in.shape[0]):
        x = _layer(x, mem, mix_norm_g[l], mem_norm_g[l], ffn_norm_g[l], w_in[l], w_mem_kv[l],
                   moba_q_norm_g[l], moba_k_norm_g[l], mem_q_norm_g[l], mem_k_norm_g[l],
                   w_up_sb[l], w_up_moba[l], w_up_mem[l], w_out[l], w_ffn_in[l], w_ffn_down[l])
    return x
```

```python
import functools
import math

import jax
import jax.numpy as jnp
from jax import lax
from jax.experimental import pallas as pl
from jax.experimental.pallas import tpu as pltpu

F32 = jnp.float32
BF16 = jnp.bfloat16

HEAD_DIM = 64
SB_HEADS = 6
MOBA_HEADS = 6
MEM_HEADS = 4
SB_W = SB_HEADS * HEAD_DIM
MOBA_W = MOBA_HEADS * HEAD_DIM
MEM_W = MEM_HEADS * HEAD_DIM
QKV_W = 3 * SB_W + 3 * MOBA_W + MEM_W
MOBA_BLOCK = 256
MOBA_TOPK = 3
ROPE_THETA = 10000.0
EPS = 1e-6
QK_SCALE = 1.0 / math.sqrt(HEAD_DIM)

LANES = 128
BF16_SUBLANES = 16
VT_ROWS = HEAD_DIM + BF16_SUBLANES
NEG_BIG = -1e30

SB_DONE_LOG = -50.0

INPROJ_ROWS = 512
SB_ROWS = 256
SB_SUB = 128
SB_WIN = 256
SB_LOOKAHEAD = 2
MOBA_LOOKAHEAD = 2
MOBA_QBLOCKS = 2
MOBA_SAFE_LOGIT = 40.0
NORM_SQ_MARGIN = 1.05
MEMATTN_ROWS = 1024
TAIL_ROWS = 256
VMEM_LIMIT = 56 * 1024 * 1024


def _dot(a, b):
    return jnp.dot(a, b, preferred_element_type=F32)


def _dot_nt(a, b):
    return lax.dot_general(a, b, (((1,), (1,)), ((), ())), preferred_element_type=F32)


def _split_bf16(t):
    hi = t.astype(BF16)
    lo = (t - hi.astype(F32)).astype(BF16)
    return hi, lo


def _rms_rows(x, g):
    ms = jnp.mean(x * x, axis=-1, keepdims=True)
    return x * lax.rsqrt(ms + EPS) * g


def _head_rms(t, bd, g):
    hi, lo = _split_bf16(t * t)
    ms = _dot(hi, bd) + _dot(lo, bd)
    return t * lax.rsqrt(ms + EPS) * g


def _rotate_half_pairs(t):
    lane = lax.broadcasted_iota(jnp.int32, t.shape, 1)
    first_half = (lane & (HEAD_DIM - 1)) < (HEAD_DIM // 2)
    return jnp.where(first_half, pltpu.roll(t, LANES - HEAD_DIM // 2, 1), pltpu.roll(t, HEAD_DIM // 2, 1))


def _rope(t, cos, sin_signed):
    outs = []
    for c in range(t.shape[1] // LANES):
        tc = t[:, c * LANES:(c + 1) * LANES]
        outs.append(tc * cos + _rotate_half_pairs(tc) * sin_signed)
    return jnp.concatenate(outs, axis=1)


def _memkv_kernel(mem_ref, g_ref, w_ref, bd_ref, gk_ref, ck_ref, cv_ref):
    h = _rms_rows(mem_ref[0], g_ref[...]).astype(BF16)
    kv = _dot(h, w_ref[...])
    k = kv[:, :MEM_W]
    ck_ref[0] = _head_rms(k, bd_ref[...], gk_ref[...]).astype(BF16)
    cv_ref[0] = kv[:, MEM_W:].astype(BF16)


def _memkv(mem, g, w, bd, gk):
    b, m, d = mem.shape
    return pl.pallas_call(
        _memkv_kernel,
        grid=(b,),
        in_specs=[
            pl.BlockSpec((1, m, d), lambda i: (i, 0, 0)),
            pl.BlockSpec((1, d), lambda i: (0, 0)),
            pl.BlockSpec((d, 2 * MEM_W), lambda i: (0, 0)),
            pl.BlockSpec((MEM_W, MEM_W), lambda i: (0, 0)),
            pl.BlockSpec((1, MEM_W), lambda i: (0, 0)),
        ],
        out_specs=[
            pl.BlockSpec((1, m, MEM_W), lambda i: (i, 0, 0)),
            pl.BlockSpec((1, m, MEM_W), lambda i: (i, 0, 0)),
        ],
        out_shape=[jax.ShapeDtypeStruct((b, m, MEM_W), BF16)] * 2,
        compiler_params=pltpu.CompilerParams(dimension_semantics=("arbitrary",), vmem_limit_bytes=VMEM_LIMIT),
        name="memkv",
    )(mem, g, w, bd, gk)


def _inproj_kernel(x_ref, g_ref, w_ref, cos_ref, sin_ref, bd_ref, gq_ref, gk_ref, gmq_ref,
                   sbq_ref, sbk_ref, sbv_ref, moq_ref, mok_ref, movt_ref, meq_ref, kmean_ref, qnorm_ref, knorm_ref):
    h = _rms_rows(x_ref[...], g_ref[...]).astype(BF16)
    proj = _dot(h, w_ref[...])
    o = 0
    sbq_ref[...] = (proj[:, o:o + SB_W] * QK_SCALE).astype(BF16); o += SB_W
    sbk_ref[...] = proj[:, o:o + SB_W].astype(BF16); o += SB_W
    sbv_ref[...] = proj[:, o:o + SB_W].astype(BF16); o += SB_W
    bd = bd_ref[...]
    cos = cos_ref[...]
    sin = sin_ref[...]
    mq = _rope(_head_rms(proj[:, o:o + MOBA_W], bd, gq_ref[...]), cos, sin); o += MOBA_W
    mq16 = (mq * QK_SCALE).astype(BF16)
    moq_ref[...] = mq16
    mk = _rope(_head_rms(proj[:, o:o + MOBA_W], bd, gk_ref[...]), cos, sin); o += MOBA_W
    mk16 = mk.astype(BF16)
    mok_ref[...] = mk16

    def row_norm_sq_bound(t16):
        t = t16.astype(F32)
        return _dot((t * t).astype(BF16), bd) * (HEAD_DIM * NORM_SQ_MARGIN)

    qn, kn = row_norm_sq_bound(mq16), row_norm_sq_bound(mk16)
    for blk in range(INPROJ_ROWS // MOBA_BLOCK):
        rows = slice(blk * MOBA_BLOCK, (blk + 1) * MOBA_BLOCK)
        kmean_ref[0, blk:blk + 1, :] = jnp.mean(mk[rows], axis=0, keepdims=True)
        qnorm_ref[0, blk:blk + 1, :] = jnp.max(qn[rows], axis=0, keepdims=True)
        knorm_ref[0, blk:blk + 1, :] = jnp.max(kn[rows], axis=0, keepdims=True)
    vt = proj[:, o:o + MOBA_W].T; o += MOBA_W
    for hd in range(MOBA_HEADS):
        movt_ref[0, hd, 0:HEAD_DIM, :] = vt[hd * HEAD_DIM:(hd + 1) * HEAD_DIM, :].astype(BF16)
        movt_ref[0, hd, HEAD_DIM:, :] = jnp.ones((VT_ROWS - HEAD_DIM, vt.shape[1]), BF16)
    cq = _head_rms(proj[:, o:o + MEM_W], bd[:MEM_W, :MEM_W], gmq_ref[...])
    meq_ref[...] = (cq * QK_SCALE).astype(BF16)


def _inproj(x2, g, w, cos, sin, bd, gq, gk, gmq, seq):
    n, d = x2.shape
    tm = INPROJ_ROWS
    steps = n // tm
    per_seq = seq // tm
    blocks_per_step = tm // MOBA_BLOCK
    const = lambda i: (0, 0)
    rows = lambda i: (i, 0)
    out_shapes = [jax.ShapeDtypeStruct((n, SB_W), BF16)] * 3 + [jax.ShapeDtypeStruct((n, MOBA_W), BF16)] * 2 + [
        jax.ShapeDtypeStruct((n // seq, MOBA_HEADS, VT_ROWS, seq), BF16),
        jax.ShapeDtypeStruct((n, MEM_W), BF16),
    ] + [jax.ShapeDtypeStruct((steps, blocks_per_step, MOBA_W), F32)] * 3
    out_specs = [pl.BlockSpec((tm, SB_W), rows)] * 3 + [pl.BlockSpec((tm, MOBA_W), rows)] * 2 + [
        pl.BlockSpec((1, MOBA_HEADS, VT_ROWS, tm), lambda i: (i // per_seq, 0, 0, i % per_seq)),
        pl.BlockSpec((tm, MEM_W), rows),
    ] + [pl.BlockSpec((1, blocks_per_step, MOBA_W), lambda i: (i, 0, 0))] * 3
    return pl.pallas_call(
        _inproj_kernel,
        grid=(steps,),
        in_specs=[
            pl.BlockSpec((tm, d), rows),
            pl.BlockSpec((1, d), const),
            pl.BlockSpec((d, QKV_W), const),
            pl.BlockSpec((tm, LANES), lambda i: (i % per_seq, 0)),
            pl.BlockSpec((tm, LANES), lambda i: (i % per_seq, 0)),
            pl.BlockSpec((MOBA_W, MOBA_W), const),
            pl.BlockSpec((1, MOBA_W), const),
            pl.BlockSpec((1, MOBA_W), const),
            pl.BlockSpec((1, MEM_W), const),
        ],
        out_specs=out_specs,
        out_shape=out_shapes,
        compiler_params=pltpu.CompilerParams(dimension_semantics=("arbitrary",), vmem_limit_bytes=VMEM_LIMIT),
        name="inproj",
    )(x2, g, w, cos, sin, bd, gq, gk, gmq)


def _sb_kernel(q_ref, k_ref, v_ref, o_ref):
    i = pl.program_id(1)
    lane = lax.broadcasted_iota(jnp.int32, (SB_SUB, LANES), 1)
    low = lane < HEAD_DIM
    kiota = lax.broadcasted_iota(jnp.int32, (SB_SUB, SB_WIN), 1)
    riota = lax.broadcasted_iota(jnp.int32, (SB_SUB, 1), 0)
    ur = lax.broadcasted_iota(jnp.int32, (SB_WIN, SB_WIN), 0)
    uc = lax.broadcasted_iota(jnp.int32, (SB_WIN, SB_WIN), 1)
    later = (ur > uc).astype(BF16)
    n_pairs = SB_W // LANES

    n_sub = SB_ROWS // SB_SUB
    chains = []
    for sub in range(n_sub):
        for pair in range(n_pairs):
            q2 = q_ref[0, sub * SB_SUB:(sub + 1) * SB_SUB, pair * LANES:(pair + 1) * LANES]
            zero = jnp.zeros_like(q2)
            chains += [(sub, pair, jnp.where(low, q2, zero)), (sub, pair, jnp.where(low, zero, q2))]
    n_chains = len(chains)

    def scan_windows(starts, limits, carries):
        def cols(c):
            return slice(chains[c][1] * LANES, (chains[c][1] + 1) * LANES)

        def scores(c):
            sub = chains[c][0]
            return _dot_nt(chains[c][2], k_ref[0, pl.ds(starts[sub], SB_WIN), cols(c)])

        def scan(c, z):
            sub = chains[c][0]
            valid = (starts[sub] + kiota) < limits[sub]
            log_beta = jnp.minimum(z, 0.0) - jnp.log(1.0 + jnp.exp(-jnp.abs(z)))
            log_1m = jnp.where(valid, log_beta - z, 0.0)
            hi, lo = _split_bf16(log_1m)
            tail = _dot(hi, later) + _dot(lo, later)
            return valid, log_beta, tail, carries[c] + jnp.sum(log_1m, axis=1, keepdims=True)

        def weigh(c, valid, log_beta, tail):
            sub = chains[c][0]
            a = jnp.where(valid, jnp.exp(log_beta + tail + carries[c]), 0.0)
            return _dot(a.astype(BF16), v_ref[0, pl.ds(starts[sub], SB_WIN), cols(c)])

        zs, scans, outs = {}, {}, [None] * n_chains
        for t in range(n_chains + 2 * SB_LOOKAHEAD):
            if t < n_chains:
                zs[t] = scores(t)
            c = t - SB_LOOKAHEAD
            if 0 <= c < n_chains:
                scans[c] = scan(c, zs.pop(c))
            c = t - 2 * SB_LOOKAHEAD
            if 0 <= c < n_chains:
                valid, log_beta, tail, carry = scans.pop(c)
                outs[c] = (weigh(c, valid, log_beta, tail), carry)
        return outs

    def unfinished(cs):
        worst = cs[0]
        for c in cs[1:]:
            worst = jnp.maximum(worst, c)
        return (jnp.max(worst) > SB_DONE_LOG).astype(jnp.int32)

    t0s = [i * SB_ROWS + sub * SB_SUB for sub in range(n_sub)]
    starts0 = [pl.multiple_of(jnp.maximum(t0 - (SB_WIN - SB_SUB), 0), SB_SUB) for t0 in t0s]
    first = scan_windows(starts0, [t0 + riota for t0 in t0s], [jnp.zeros((SB_SUB, 1), F32)] * n_chains)
    accs = [acc for acc, _ in first]
    carries = [carry for _, carry in first]

    def cond(state):
        ends, more = state[:n_sub], state[n_sub]
        remaining = ends[0]
        for e in ends[1:]:
            remaining = jnp.maximum(remaining, e)
        return jnp.logical_and(remaining > 0, more > 0)

    def body(state):
        ends = state[:n_sub]
        accs_c = state[n_sub + 1:n_sub + 1 + n_chains]
        cs_c = state[n_sub + 1 + n_chains:]
        starts = [pl.multiple_of(jnp.maximum(e - SB_WIN, 0), SB_SUB) for e in ends]
        limits = [jnp.full((SB_SUB, 1), e, jnp.int32) for e in ends]
        res = scan_windows(starts, limits, list(cs_c))
        new_accs = [a + r[0] for a, r in zip(accs_c, res)]
        new_cs = [r[1] for r in res]
        return (*starts, unfinished(new_cs), *new_accs, *new_cs)

    state = lax.while_loop(cond, body, (*starts0, unfinished(carries), *accs, *carries))
    accs = state[n_sub + 1:n_sub + 1 + n_chains]
    for sub in range(n_sub):
        for pair in range(n_pairs):
            c = (sub * n_pairs + pair) * 2
            out = jnp.where(low, accs[c], accs[c + 1])
            o_ref[0, sub * SB_SUB:(sub + 1) * SB_SUB, pair * LANES:(pair + 1) * LANES] = out.astype(BF16)


def _sb_attention(q, k, v):
    b, s, w = q.shape
    return pl.pallas_call(
        _sb_kernel,
        grid=(b, s // SB_ROWS),
        in_specs=[
            pl.BlockSpec((1, SB_ROWS, w), lambda bi, i: (bi, i, 0)),
            pl.BlockSpec((1, s, w), lambda bi, i: (bi, 0, 0)),
            pl.BlockSpec((1, s, w), lambda bi, i: (bi, 0, 0)),
        ],
        out_specs=pl.BlockSpec((1, SB_ROWS, w), lambda bi, i: (bi, i, 0)),
        out_shape=jax.ShapeDtypeStruct((b, s, w), BF16),
        compiler_params=pltpu.CompilerParams(dimension_semantics=("arbitrary", "arbitrary"),
                                             vmem_limit_bytes=VMEM_LIMIT),
        name="sb_attention",
    )(q, k, v)


def _moba_kernel(q_ref, k_ref, vt_ref, km_ref, qn_ref, kn_ref, o_ref):
    first = pl.program_id(1) * MOBA_QBLOCKS
    nb = km_ref.shape[1]
    tq = MOBA_BLOCK
    n_heads = vt_ref.shape[1]
    lane = lax.broadcasted_iota(jnp.int32, (tq, LANES), 1)
    low = lane < HEAD_DIM
    blk = lax.broadcasted_iota(jnp.int32, (nb, tq), 0)
    items = [(qb, h) for qb in range(MOBA_QBLOCKS) for h in range(n_heads)]

    def cols(h):
        pair = h // 2
        return slice(pair * LANES, (pair + 1) * LANES)

    qms = []
    for qb in range(MOBA_QBLOCKS):
        for pair in range(n_heads // 2):
            q2 = q_ref[0, qb * tq:(qb + 1) * tq, pair * LANES:(pair + 1) * LANES]
            zero = jnp.zeros_like(q2)
            qms += [jnp.where(low, q2, zero), jnp.where(low, zero, q2)]

    km_split = [_split_bf16(km_ref[0, :, cols(h)]) for h in range(n_heads)]
    gates = [_dot_nt(km_split[h][0], qms[it]) + _dot_nt(km_split[h][1], qms[it])
             for it, (_, h) in enumerate(items)]

    def select(it):
        gate = jnp.where(blk < first + items[it][0], gates[it], -jnp.inf)
        bits = jnp.zeros((1, tq), jnp.int32)
        for _ in range(MOBA_TOPK):
            best = jnp.max(gate, axis=0, keepdims=True)
            idx = jnp.min(jnp.where(gate == best, blk, nb), axis=0, keepdims=True)
            bits = bits | jnp.where(best > -jnp.inf, jnp.left_shift(1, idx), 0)
            gate = jnp.where(blk == idx, -jnp.inf, gate)
        return bits

    bits = [select(it) for it in range(len(items))]

    def block_offset(j):
        return pl.multiple_of(j * tq, tq)

    def pipelined(work, scores, reduce):
        queued = {w: scores(w) for w in work[:MOBA_LOOKAHEAD]}
        for n, w in enumerate(work):
            if n + MOBA_LOOKAHEAD < len(work):
                ahead = work[n + MOBA_LOOKAHEAD]
                queued[ahead] = scores(ahead)
            reduce(w, queued.pop(w))

    causal = lax.broadcasted_iota(jnp.int32, (tq, tq), 0) <= lax.broadcasted_iota(jnp.int32, (tq, tq), 1)
    n_items = len(items)
    all_items = list(range(n_items))

    def own_scores(it):
        qb, h = items[it]
        return _dot_nt(k_ref[0, pl.ds(block_offset(first + qb), tq), cols(h)], qms[it])

    def past_scores(blocks_of):
        def scores(it):
            h = items[it][1]
            return [_dot_nt(k_ref[0, pl.ds(block_offset(j), tq), cols(h)], qms[it]) for j in blocks_of(it)]
        return scores

    def v_rows(it, j):
        return vt_ref[0, items[it][1], :, pl.ds(block_offset(j), tq)]

    def picked(it, j):
        return (jnp.right_shift(bits[it], j) & 1) > 0

    def earlier_in_step(it):
        return [first + e for e in range(items[it][0])]

    def loop_blocks(jj):
        return lambda it: [MOBA_QBLOCKS * jj + u for u in range(MOBA_QBLOCKS)]

    def with_earlier(qb):
        return [it for it in all_items if items[it][0] == qb]

    def stable_path():
        state = [None] * (2 * n_items)

        def own_reduce(it, raw):
            s = jnp.where(causal, raw, NEG_BIG)
            m = jnp.max(s, axis=0, keepdims=True)
            state[2 * it] = m
            state[2 * it + 1] = _dot(v_rows(it, first + items[it][0]), jnp.exp(s - m).astype(BF16))

        pipelined(all_items, own_scores, own_reduce)

        def attend(st, work, blocks_of):
            def reduce(it, raws):
                m, acc = st[2 * it], st[2 * it + 1]
                picks = [picked(it, j) for j in blocks_of(it)]
                m_new = m
                for pick, s in zip(picks, raws):
                    m_new = jnp.maximum(m_new, jnp.where(pick, jnp.max(s, axis=0, keepdims=True), NEG_BIG))
                acc = jnp.exp(m - m_new) * acc
                for j, pick, s in zip(blocks_of(it), picks, raws):
                    p = jnp.exp((s - m_new).astype(BF16))
                    acc = acc + _dot(v_rows(it, j), jnp.where(pick, p, jnp.zeros_like(p)))
                st[2 * it] = m_new
                st[2 * it + 1] = acc
            pipelined(work, past_scores(blocks_of), reduce)

        for qb in range(1, MOBA_QBLOCKS):
            attend(state, with_earlier(qb), earlier_in_step)

        def body(jj, st):
            st = list(st)
            attend(st, all_items, loop_blocks(jj))
            return tuple(st)

        state = lax.fori_loop(0, pl.program_id(1), body, tuple(state))
        return tuple(state[1::2])

    def bounded_path():
        accs = [None] * n_items

        def own_reduce(it, raw):
            p = jnp.where(causal, jnp.exp(raw), 0.0).astype(BF16)
            accs[it] = _dot(v_rows(it, first + items[it][0]), p)

        pipelined(all_items, own_scores, own_reduce)

        def attend(st, work, blocks_of):
            def reduce(it, raws):
                acc = st[it]
                for j, s in zip(blocks_of(it), raws):
                    p = jnp.exp(s).astype(BF16)
                    acc = acc + _dot(v_rows(it, j), jnp.where(picked(it, j), p, jnp.zeros_like(p)))
                st[it] = acc
            pipelined(work, past_scores(blocks_of), reduce)

        for qb in range(1, MOBA_QBLOCKS):
            attend(accs, with_earlier(qb), earlier_in_step)

        def body(jj, st):
            st = list(st)
            attend(st, all_items, loop_blocks(jj))
            return tuple(st)

        return lax.fori_loop(0, pl.program_id(1), body, tuple(accs))

    score_bound_sq = jnp.max(qn_ref[0]) * jnp.max(kn_ref[0])
    accs = lax.cond(score_bound_sq <= MOBA_SAFE_LOGIT * MOBA_SAFE_LOGIT, bounded_path, stable_path)
    for qb in range(MOBA_QBLOCKS):
        for pair in range(n_heads // 2):
            outs = []
            for h in (2 * pair, 2 * pair + 1):
                acc = accs[qb * n_heads + h]
                outs.append(acc[:HEAD_DIM] / acc[HEAD_DIM:HEAD_DIM + 1])
            o_ref[0, qb * tq:(qb + 1) * tq, pair * LANES:(pair + 1) * LANES] = (
                jnp.concatenate(outs, axis=0).T.astype(BF16))


def _moba_attention(q, k, vt, kmean, qnorm, knorm):
    b, s, w = q.shape
    nb = s // MOBA_BLOCK
    rows = MOBA_QBLOCKS * MOBA_BLOCK
    return pl.pallas_call(
        _moba_kernel,
        grid=(b, nb // MOBA_QBLOCKS),
        in_specs=[
            pl.BlockSpec((1, rows, w), lambda bi, i: (bi, i, 0)),
            pl.BlockSpec((1, s, w), lambda bi, i: (bi, 0, 0)),
            pl.BlockSpec((1, w // HEAD_DIM, VT_ROWS, s), lambda bi, i: (bi, 0, 0, 0)),
            pl.BlockSpec((1, nb, w), lambda bi, i: (bi, 0, 0)),
            pl.BlockSpec((1, MOBA_QBLOCKS, w), lambda bi, i: (bi * (nb // MOBA_QBLOCKS) + i, 0, 0)),
            pl.BlockSpec((1, nb, w), lambda bi, i: (bi, 0, 0)),
        ],
        out_specs=pl.BlockSpec((1, rows, w), lambda bi, i: (bi, i, 0)),
        out_shape=jax.ShapeDtypeStruct((b, s, w), BF16),
        compiler_params=pltpu.CompilerParams(dimension_semantics=("arbitrary",) * 2, vmem_limit_bytes=VMEM_LIMIT),
        name="moba_attention",
    )(q, k, vt, kmean, qnorm.reshape(b * nb // MOBA_QBLOCKS, MOBA_QBLOCKS, w), knorm)


def _memattn_kernel(q_ref, ck_ref, cv_ref, o_ref):
    rows = q_ref.shape[1]
    lane = lax.broadcasted_iota(jnp.int32, (rows, LANES), 1)
    low = lane < HEAD_DIM
    for pair in range(MEM_W // LANES):
        cols = slice(pair * LANES, (pair + 1) * LANES)
        q2 = q_ref[0, :, cols]
        kp = ck_ref[0, :, cols]
        vp = cv_ref[0, :, cols]
        zero = jnp.zeros_like(q2)
        outs = []
        for qm in (jnp.where(low, q2, zero), jnp.where(low, zero, q2)):
            s = _dot_nt(qm, kp)
            p = jnp.exp(s - jnp.max(s, axis=1, keepdims=True))
            outs.append(_dot(p.astype(BF16), vp) / jnp.sum(p, axis=1, keepdims=True))
        o_ref[0, :, cols] = jnp.where(low, outs[0], outs[1]).astype(BF16)


def _mem_attention(q, ck, cv):
    b, s, w = q.shape
    m = ck.shape[1]
    tm = MEMATTN_ROWS
    return pl.pallas_call(
        _memattn_kernel,
        grid=(b, s // tm),
        in_specs=[
            pl.BlockSpec((1, tm, w), lambda bi, i: (bi, i, 0)),
            pl.BlockSpec((1, m, w), lambda bi, i: (bi, 0, 0)),
            pl.BlockSpec((1, m, w), lambda bi, i: (bi, 0, 0)),
        ],
        out_specs=pl.BlockSpec((1, tm, w), lambda bi, i: (bi, i, 0)),
        out_shape=jax.ShapeDtypeStruct((b, s, w), BF16),
        compiler_params=pltpu.CompilerParams(dimension_semantics=("arbitrary", "arbitrary"),
                                             vmem_limit_bytes=VMEM_LIMIT),
        name="mem_attention",
    )(q, ck, cv)


def _sigmoid(t):
    return 1.0 / (1.0 + jnp.exp(-t))


def _tail_kernel(x_ref, osb_ref, omo_ref, ome_ref, gmix_ref, gffn_ref, wg_ref, wsb_ref, wmo_ref, wme_ref,
                 wout_ref, wfi_ref, wfd_ref, out_ref):
    x = x_ref[...]
    d = x.shape[1]
    h = _rms_rows(x, gmix_ref[...]).astype(BF16)
    mix = None
    for n, (o_ref, wu_ref) in enumerate(((osb_ref, wsb_ref), (omo_ref, wmo_ref), (ome_ref, wme_ref))):
        gate = _dot(h, wg_ref[:, n * d:(n + 1) * d])
        term = _sigmoid(gate) * _dot(o_ref[...], wu_ref[...])
        mix = term if mix is None else mix + term
    x1 = x + _dot(mix.astype(BF16), wout_ref[...])
    h2 = _rms_rows(x1, gffn_ref[...]).astype(BF16)
    dff = wfd_ref.shape[0]
    gate = _dot(h2, wfi_ref[:, :dff])
    up = _dot(h2, wfi_ref[:, dff:])
    ff = (gate * _sigmoid(gate) * up).astype(BF16)
    out_ref[...] = x1 + _dot(ff, wfd_ref[...])


def _tail(x2, osb, omo, ome, gmix, gffn, wg, wsb, wmo, wme, wout, wfi, wfd):
    n, d = x2.shape
    tm = TAIL_ROWS
    rows = lambda i: (i, 0)

    def resident(arr):
        return pl.BlockSpec(arr.shape, lambda i: (0, 0), pipeline_mode=pl.Buffered(1))

    return pl.pallas_call(
        _tail_kernel,
        grid=(n // tm,),
        in_specs=[
            pl.BlockSpec((tm, d), rows),
            pl.BlockSpec((tm, SB_W), rows),
            pl.BlockSpec((tm, MOBA_W), rows),
            pl.BlockSpec((tm, MEM_W), rows),
            resident(gmix), resident(gffn), resident(wg), resident(wsb), resident(wmo), resident(wme),
            resident(wout), resident(wfi), resident(wfd),
        ],
        out_specs=pl.BlockSpec((tm, d), rows),
        out_shape=jax.ShapeDtypeStruct((n, d), F32),
        compiler_params=pltpu.CompilerParams(dimension_semantics=("arbitrary",), vmem_limit_bytes=VMEM_LIMIT),
        name="merge_out_ffn",
    )(x2, osb, omo, ome, gmix, gffn, wg, wsb, wmo, wme, wout, wfi, wfd)


def _rope_tables(seq):
    half = HEAD_DIM // 2
    inv_freq = ROPE_THETA ** (-jnp.arange(half, dtype=F32) * 2.0 / HEAD_DIM)
    ang = jnp.arange(seq, dtype=F32)[:, None] * inv_freq[None, :]
    cos, sin = jnp.cos(ang), jnp.sin(ang)
    reps = LANES // HEAD_DIM
    return jnp.tile(jnp.concatenate([cos, cos], axis=1), (1, reps)), jnp.tile(jnp.concatenate([-sin, sin], axis=1), (1, reps))


def _head_mean_matrix(width):
    r = jnp.arange(width) // HEAD_DIM
    return jnp.where(r[:, None] == r[None, :], 1.0 / HEAD_DIM, 0.0).astype(BF16)


def _layer(x, mem, mix_norm_g, mem_norm_g, ffn_norm_g, w_in, w_mem_kv, moba_q_norm_g, moba_k_norm_g,
           mem_q_norm_g, mem_k_norm_g, w_up_sb, w_up_moba, w_up_mem, w_out, w_ffn_in, w_ffn_down):
    b, s, d = x.shape
    x2 = x.reshape(b * s, d)
    cos, sin = _rope_tables(s)
    bd = _head_mean_matrix(MOBA_W)
    row = lambda g, reps=1: jnp.tile(g.astype(F32), reps)[None, :]

    ck, cv = _memkv(mem, row(mem_norm_g), w_mem_kv.astype(BF16), bd[:MEM_W, :MEM_W], row(mem_k_norm_g, MEM_HEADS))
    sbq, sbk, sbv, moq, mok, movt, meq, kmean, qnorm, knorm = _inproj(
        x2, row(mix_norm_g), w_in[:, :QKV_W].astype(BF16), cos, sin, bd,
        row(moba_q_norm_g, MOBA_HEADS), row(moba_k_norm_g, MOBA_HEADS), row(mem_q_norm_g, MEM_HEADS), s)
    seq3 = lambda t: t.reshape(b, s, t.shape[-1])
    o_sb = _sb_attention(seq3(sbq), seq3(sbk), seq3(sbv))
    per_block = lambda t: t.reshape(b, s // MOBA_BLOCK, MOBA_W)
    o_mo = _moba_attention(seq3(moq), seq3(mok), movt, per_block(kmean), per_block(qnorm), per_block(knorm))
    o_me = _mem_attention(seq3(meq), ck, cv)
    flat = lambda t: t.reshape(b * s, t.shape[-1])
    out = _tail(x2, flat(o_sb), flat(o_mo), flat(o_me), row(mix_norm_g), row(ffn_norm_g),
                w_in[:, QKV_W:].astype(BF16), w_up_sb.astype(BF16), w_up_moba.astype(BF16), w_up_mem.astype(BF16),
                w_out.astype(BF16), w_ffn_in.astype(BF16), w_ffn_down.astype(BF16))
    return out.reshape(b, s, d)


def kernel(x, mem, mix_norm_g, mem_norm_g, ffn_norm_g, w_in, w_mem_kv, moba_q_norm_g, moba_k_norm_g, mem_q_norm_g, mem_k_norm_g, w_up_sb, w_up_moba, w_up_mem, w_out, w_ffn_in, w_ffn_down):
    for l in range(w_in.shape[0]):
        x = _layer(x, mem, mix_norm_g[l], mem_norm_g[l], ffn_norm_g[l], w_in[l], w_mem_kv[l],
                   moba_q_norm_g[l], moba_k_norm_g[l], mem_q_norm_g[l], mem_k_norm_g[l],
                   w_up_sb[l], w_up_moba[l], w_up_mem[l], w_out[l], w_ffn_in[l], w_ffn_down[l])
    return x
```

```python
import functools
import math

import jax
import jax.numpy as jnp
from jax import lax
from jax.experimental import pallas as pl
from jax.experimental.pallas import tpu as pltpu

F32 = jnp.float32
BF16 = jnp.bfloat16

HEAD_DIM = 64
SB_HEADS = 6
MOBA_HEADS = 6
MEM_HEADS = 4
SB_W = SB_HEADS * HEAD_DIM
MOBA_W = MOBA_HEADS * HEAD_DIM
MEM_W = MEM_HEADS * HEAD_DIM
QKV_W = 3 * SB_W + 3 * MOBA_W + MEM_W
MOBA_BLOCK = 256
MOBA_TOPK = 3
ROPE_THETA = 10000.0
EPS = 1e-6
QK_SCALE = 1.0 / math.sqrt(HEAD_DIM)

LANES = 128
MXU_TILE = 256
BF16_SUBLANES = 16
VT_ROWS = HEAD_DIM + BF16_SUBLANES
NEG_BIG = -1e30

SB_DONE_LOG = -50.0

INPROJ_ROWS = 512
SB_ROWS = 256
SB_SUB = 128
SB_WIN = 256
SB_LOOKAHEAD = 2
MOBA_LOOKAHEAD = 2
MOBA_QBLOCKS = 4
MOBA_LOOP_BLOCKS = 4
MOBA_SAFE_LOGIT = 40.0
SCORE_BOUND_MARGIN = 1.1
MEMATTN_ROWS = 1024
TAIL_ROWS = 512
VMEM_LIMIT = 56 * 1024 * 1024


def _dot(a, b):
    return jnp.dot(a, b, preferred_element_type=F32)


def _dot_nt(a, b):
    return lax.dot_general(a, b, (((1,), (1,)), ((), ())), preferred_element_type=F32)


def _split_bf16(t):
    hi = t.astype(BF16)
    lo = (t - hi.astype(F32)).astype(BF16)
    return hi, lo


def _rms_rows(x, g):
    ms = jnp.mean(x * x, axis=-1, keepdims=True)
    return x * lax.rsqrt(ms + EPS) * g


def _head_rms(t, bd, g):
    hi, lo = _split_bf16(t * t)
    parts = []
    for c in range(0, t.shape[1], MXU_TILE):
        w = min(MXU_TILE, t.shape[1] - c)
        parts.append(_dot(hi[:, c:c + w], bd[:w, :w]) + _dot(lo[:, c:c + w], bd[:w, :w]))
    ms = parts[0] if len(parts) == 1 else jnp.concatenate(parts, axis=1)
    return t * lax.rsqrt(ms + EPS) * g


def _rotate_half_pairs(t):
    lane = lax.broadcasted_iota(jnp.int32, t.shape, 1)
    first_half = (lane & (HEAD_DIM - 1)) < (HEAD_DIM // 2)
    return jnp.where(first_half, pltpu.roll(t, LANES - HEAD_DIM // 2, 1), pltpu.roll(t, HEAD_DIM // 2, 1))


def _rope(t, cos, sin_signed):
    outs = []
    for c in range(t.shape[1] // LANES):
        tc = t[:, c * LANES:(c + 1) * LANES]
        outs.append(tc * cos + _rotate_half_pairs(tc) * sin_signed)
    return jnp.concatenate(outs, axis=1)


def _memkv_kernel(mem_ref, g_ref, w_ref, bd_ref, gk_ref, ck_ref, cv_ref):
    h = _rms_rows(mem_ref[0], g_ref[...]).astype(BF16)
    kv = _dot(h, w_ref[...])
    k = kv[:, :MEM_W]
    ck_ref[0] = _head_rms(k, bd_ref[...], gk_ref[...]).astype(BF16)
    cv_ref[0] = kv[:, MEM_W:].astype(BF16)


def _memkv(mem, g, w, bd, gk):
    b, m, d = mem.shape
    return pl.pallas_call(
        _memkv_kernel,
        grid=(b,),
        in_specs=[
            pl.BlockSpec((1, m, d), lambda i: (i, 0, 0)),
            pl.BlockSpec((1, d), lambda i: (0, 0)),
            pl.BlockSpec((d, 2 * MEM_W), lambda i: (0, 0)),
            pl.BlockSpec((MXU_TILE, MXU_TILE), lambda i: (0, 0)),
            pl.BlockSpec((1, MEM_W), lambda i: (0, 0)),
        ],
        out_specs=[
            pl.BlockSpec((1, m, MEM_W), lambda i: (i, 0, 0)),
            pl.BlockSpec((1, m, MEM_W), lambda i: (i, 0, 0)),
        ],
        out_shape=[jax.ShapeDtypeStruct((b, m, MEM_W), BF16)] * 2,
        compiler_params=pltpu.CompilerParams(dimension_semantics=("arbitrary",), vmem_limit_bytes=VMEM_LIMIT),
        name="memkv",
    )(mem, g, w, bd, gk)


def _inproj_kernel(x_ref, g_ref, w_ref, cos_ref, sin_ref, bd_ref, gqk_ref, gmq_ref,
                   sbq_ref, sbk_ref, sbv_ref, moq_ref, mok_ref, movt_ref, meq_ref, kmean_ref):
    h = _rms_rows(x_ref[...], g_ref[...]).astype(BF16)
    proj = _dot(h, w_ref[...])
    o = 0
    sbq_ref[...] = (proj[:, o:o + SB_W] * QK_SCALE).astype(BF16); o += SB_W
    sbk_ref[...] = proj[:, o:o + SB_W].astype(BF16); o += SB_W
    sbv_ref[...] = proj[:, o:o + SB_W].astype(BF16); o += SB_W
    bd = bd_ref[...]
    qk = _rope(_head_rms(proj[:, o:o + 2 * MOBA_W], bd, gqk_ref[...]), cos_ref[...], sin_ref[...]); o += 2 * MOBA_W
    moq_ref[...] = (qk[:, :MOBA_W] * QK_SCALE).astype(BF16)
    mk = qk[:, MOBA_W:]
    mok_ref[...] = mk.astype(BF16)
    for blk in range(INPROJ_ROWS // MOBA_BLOCK):
        kmean_ref[0, blk:blk + 1, :] = jnp.mean(mk[blk * MOBA_BLOCK:(blk + 1) * MOBA_BLOCK], axis=0, keepdims=True)
    vt = proj[:, o:o + MOBA_W].T; o += MOBA_W
    for hd in range(MOBA_HEADS):
        movt_ref[0, hd, 0:HEAD_DIM, :] = vt[hd * HEAD_DIM:(hd + 1) * HEAD_DIM, :].astype(BF16)
        movt_ref[0, hd, HEAD_DIM:, :] = jnp.ones((VT_ROWS - HEAD_DIM, vt.shape[1]), BF16)
    cq = _head_rms(proj[:, o:o + MEM_W], bd, gmq_ref[...])
    meq_ref[...] = (cq * QK_SCALE).astype(BF16)


def _inproj(x2, g, w, cos, sin, bd, gqk, gmq, seq):
    n, d = x2.shape
    tm = INPROJ_ROWS
    steps = n // tm
    per_seq = seq // tm
    blocks_per_step = tm // MOBA_BLOCK
    const = lambda i: (0, 0)
    rows = lambda i: (i, 0)
    out_shapes = [jax.ShapeDtypeStruct((n, SB_W), BF16)] * 3 + [jax.ShapeDtypeStruct((n, MOBA_W), BF16)] * 2 + [
        jax.ShapeDtypeStruct((n // seq, MOBA_HEADS, VT_ROWS, seq), BF16),
        jax.ShapeDtypeStruct((n, MEM_W), BF16),
        jax.ShapeDtypeStruct((steps, blocks_per_step, MOBA_W), F32),
    ]
    out_specs = [pl.BlockSpec((tm, SB_W), rows)] * 3 + [pl.BlockSpec((tm, MOBA_W), rows)] * 2 + [
        pl.BlockSpec((1, MOBA_HEADS, VT_ROWS, tm), lambda i: (i // per_seq, 0, 0, i % per_seq)),
        pl.BlockSpec((tm, MEM_W), rows),
        pl.BlockSpec((1, blocks_per_step, MOBA_W), lambda i: (i, 0, 0)),
    ]
    return pl.pallas_call(
        _inproj_kernel,
        grid=(steps,),
        in_specs=[
            pl.BlockSpec((tm, d), rows),
            pl.BlockSpec((1, d), const),
            pl.BlockSpec((d, QKV_W), const),
            pl.BlockSpec((tm, LANES), lambda i: (i % per_seq, 0)),
            pl.BlockSpec((tm, LANES), lambda i: (i % per_seq, 0)),
            pl.BlockSpec((MXU_TILE, MXU_TILE), const),
            pl.BlockSpec((1, 2 * MOBA_W), const),
            pl.BlockSpec((1, MEM_W), const),
        ],
        out_specs=out_specs,
        out_shape=out_shapes,
        compiler_params=pltpu.CompilerParams(dimension_semantics=("arbitrary",), vmem_limit_bytes=VMEM_LIMIT),
        name="inproj",
    )(x2, g, w, cos, sin, bd, gqk, gmq)


def _sb_kernel(q_ref, k_ref, v_ref, o_ref):
    i = pl.program_id(1)
    lane = lax.broadcasted_iota(jnp.int32, (SB_SUB, LANES), 1)
    low = lane < HEAD_DIM
    kiota = lax.broadcasted_iota(jnp.int32, (SB_SUB, SB_WIN), 1)
    riota = lax.broadcasted_iota(jnp.int32, (SB_SUB, 1), 0)
    ur = lax.broadcasted_iota(jnp.int32, (SB_WIN, SB_WIN), 0)
    uc = lax.broadcasted_iota(jnp.int32, (SB_WIN, SB_WIN), 1)
    later = (ur > uc).astype(BF16)
    n_pairs = SB_W // LANES

    n_sub = SB_ROWS // SB_SUB
    chains = []
    for sub in range(n_sub):
        for pair in range(n_pairs):
            q2 = q_ref[0, sub * SB_SUB:(sub + 1) * SB_SUB, pair * LANES:(pair + 1) * LANES]
            zero = jnp.zeros_like(q2)
            chains += [(sub, pair, jnp.where(low, q2, zero)), (sub, pair, jnp.where(low, zero, q2))]
    n_chains = len(chains)

    def scan_windows(starts, limits, carries):
        def cols(c):
            return slice(chains[c][1] * LANES, (chains[c][1] + 1) * LANES)

        def scores(c):
            sub = chains[c][0]
            return _dot_nt(chains[c][2], k_ref[0, pl.ds(starts[sub], SB_WIN), cols(c)])

        def scan(c, z):
            sub = chains[c][0]
            valid = (starts[sub] + kiota) < limits[sub]
            log_beta = jnp.minimum(z, 0.0) - jnp.log(1.0 + jnp.exp(-jnp.abs(z)))
            log_1m = jnp.where(valid, log_beta - z, 0.0)
            hi, lo = _split_bf16(log_1m)
            tail = _dot(hi, later) + _dot(lo, later)
            return valid, log_beta, tail, carries[c] + jnp.sum(log_1m, axis=1, keepdims=True)

        def weigh(c, valid, log_beta, tail):
            sub = chains[c][0]
            a = jnp.where(valid, jnp.exp(log_beta + tail + carries[c]), 0.0)
            return _dot(a.astype(BF16), v_ref[0, pl.ds(starts[sub], SB_WIN), cols(c)])

        zs, scans, outs = {}, {}, [None] * n_chains
        for t in range(n_chains + 2 * SB_LOOKAHEAD):
            if t < n_chains:
                zs[t] = scores(t)
            c = t - SB_LOOKAHEAD
            if 0 <= c < n_chains:
                scans[c] = scan(c, zs.pop(c))
            c = t - 2 * SB_LOOKAHEAD
            if 0 <= c < n_chains:
                valid, log_beta, tail, carry = scans.pop(c)
                outs[c] = (weigh(c, valid, log_beta, tail), carry)
        return outs

    def unfinished(cs):
        worst = cs[0]
        for c in cs[1:]:
            worst = jnp.maximum(worst, c)
        return (jnp.max(worst) > SB_DONE_LOG).astype(jnp.int32)

    t0s = [i * SB_ROWS + sub * SB_SUB for sub in range(n_sub)]
    starts0 = [pl.multiple_of(jnp.maximum(t0 - (SB_WIN - SB_SUB), 0), SB_SUB) for t0 in t0s]
    first = scan_windows(starts0, [t0 + riota for t0 in t0s], [jnp.zeros((SB_SUB, 1), F32)] * n_chains)
    accs = [acc for acc, _ in first]
    carries = [carry for _, carry in first]

    def cond(state):
        ends, more = state[:n_sub], state[n_sub]
        remaining = ends[0]
        for e in ends[1:]:
            remaining = jnp.maximum(remaining, e)
        return jnp.logical_and(remaining > 0, more > 0)

    def body(state):
        ends = state[:n_sub]
        accs_c = state[n_sub + 1:n_sub + 1 + n_chains]
        cs_c = state[n_sub + 1 + n_chains:]
        starts = [pl.multiple_of(jnp.maximum(e - SB_WIN, 0), SB_SUB) for e in ends]
        limits = [jnp.full((SB_SUB, 1), e, jnp.int32) for e in ends]
        res = scan_windows(starts, limits, list(cs_c))
        new_accs = [a + r[0] for a, r in zip(accs_c, res)]
        new_cs = [r[1] for r in res]
        return (*starts, unfinished(new_cs), *new_accs, *new_cs)

    state = lax.while_loop(cond, body, (*starts0, unfinished(carries), *accs, *carries))
    accs = state[n_sub + 1:n_sub + 1 + n_chains]
    for sub in range(n_sub):
        for pair in range(n_pairs):
            c = (sub * n_pairs + pair) * 2
            out = jnp.where(low, accs[c], accs[c + 1])
            o_ref[0, sub * SB_SUB:(sub + 1) * SB_SUB, pair * LANES:(pair + 1) * LANES] = out.astype(BF16)


def _sb_attention(q, k, v):
    b, s, w = q.shape
    return pl.pallas_call(
        _sb_kernel,
        grid=(b, s // SB_ROWS),
        in_specs=[
            pl.BlockSpec((1, SB_ROWS, w), lambda bi, i: (bi, i, 0)),
            pl.BlockSpec((1, s, w), lambda bi, i: (bi, 0, 0)),
            pl.BlockSpec((1, s, w), lambda bi, i: (bi, 0, 0)),
        ],
        out_specs=pl.BlockSpec((1, SB_ROWS, w), lambda bi, i: (bi, i, 0)),
        out_shape=jax.ShapeDtypeStruct((b, s, w), BF16),
        compiler_params=pltpu.CompilerParams(dimension_semantics=("arbitrary", "arbitrary"),
                                             vmem_limit_bytes=VMEM_LIMIT),
        name="sb_attention",
    )(q, k, v)


def _moba_kernel(q_ref, k_ref, vt_ref, km_ref, gqk_ref, o_ref):
    first = pl.program_id(1) * MOBA_QBLOCKS
    nb = km_ref.shape[1]
    tq = MOBA_BLOCK
    n_heads = vt_ref.shape[1]
    lane = lax.broadcasted_iota(jnp.int32, (tq, LANES), 1)
    low = lane < HEAD_DIM
    blk = lax.broadcasted_iota(jnp.int32, (nb, tq), 0)
    items = [(qb, h) for qb in range(MOBA_QBLOCKS) for h in range(n_heads)]

    def cols(h):
        pair = h // 2
        return slice(pair * LANES, (pair + 1) * LANES)

    qms = []
    for qb in range(MOBA_QBLOCKS):
        for pair in range(n_heads // 2):
            q2 = q_ref[0, qb * tq:(qb + 1) * tq, pair * LANES:(pair + 1) * LANES]
            zero = jnp.zeros_like(q2)
            qms += [jnp.where(low, q2, zero), jnp.where(low, zero, q2)]

    km_split = [_split_bf16(km_ref[0, :, cols(h)]) for h in range(n_heads)]
    gates = [_dot_nt(km_split[h][0], qms[it]) + _dot_nt(km_split[h][1], qms[it])
             for it, (_, h) in enumerate(items)]

    def select(it):
        gate = jnp.where(blk < first + items[it][0], gates[it], -jnp.inf)
        bits = jnp.zeros((1, tq), jnp.int32)
        for _ in range(MOBA_TOPK):
            best = jnp.max(gate, axis=0, keepdims=True)
            idx = jnp.min(jnp.where(gate == best, blk, nb), axis=0, keepdims=True)
            bits = bits | jnp.where(best > -jnp.inf, jnp.left_shift(1, idx), 0)
            gate = jnp.where(blk == idx, -jnp.inf, gate)
        return bits

    bits = [select(it) for it in range(len(items))]

    def block_offset(j):
        return pl.multiple_of(j * tq, tq)

    def pipelined(work, scores, reduce):
        queued = {w: scores(w) for w in work[:MOBA_LOOKAHEAD]}
        for n, w in enumerate(work):
            if n + MOBA_LOOKAHEAD < len(work):
                ahead = work[n + MOBA_LOOKAHEAD]
                queued[ahead] = scores(ahead)
            reduce(w, queued.pop(w))

    causal = lax.broadcasted_iota(jnp.int32, (tq, tq), 0) <= lax.broadcasted_iota(jnp.int32, (tq, tq), 1)
    n_items = len(items)
    all_items = list(range(n_items))

    def own_scores(it):
        qb, h = items[it]
        return _dot_nt(k_ref[0, pl.ds(block_offset(first + qb), tq), cols(h)], qms[it])

    def past_scores(blocks_of):
        def scores(it):
            h = items[it][1]
            return [_dot_nt(k_ref[0, pl.ds(block_offset(j), tq), cols(h)], qms[it]) for j in blocks_of(it)]
        return scores

    def v_rows(it, j):
        return vt_ref[0, items[it][1], :, pl.ds(block_offset(j), tq)]

    def picked(it, j):
        return (jnp.right_shift(bits[it], j) & 1) > 0

    def earlier_in_step(it):
        return [first + e for e in range(items[it][0])]

    def loop_blocks(jj):
        return lambda it: [MOBA_LOOP_BLOCKS * jj + u for u in range(MOBA_LOOP_BLOCKS)]

    loop_trips = first // MOBA_LOOP_BLOCKS

    def with_earlier(qb):
        return [it for it in all_items if items[it][0] == qb]

    def stable_path():
        state = [None] * (2 * n_items)

        def own_reduce(it, raw):
            s = jnp.where(causal, raw, NEG_BIG)
            m = jnp.max(s, axis=0, keepdims=True)
            state[2 * it] = m
            state[2 * it + 1] = _dot(v_rows(it, first + items[it][0]), jnp.exp(s - m).astype(BF16))

        pipelined(all_items, own_scores, own_reduce)

        def attend(st, work, blocks_of):
            def reduce(it, raws):
                m, acc = st[2 * it], st[2 * it + 1]
                picks = [picked(it, j) for j in blocks_of(it)]
                m_new = m
                for pick, s in zip(picks, raws):
                    m_new = jnp.maximum(m_new, jnp.where(pick, jnp.max(s, axis=0, keepdims=True), NEG_BIG))
                acc = jnp.exp(m - m_new) * acc
                for j, pick, s in zip(blocks_of(it), picks, raws):
                    p = jnp.exp((s - m_new).astype(BF16))
                    acc = acc + _dot(v_rows(it, j), jnp.where(pick, p, jnp.zeros_like(p)))
                st[2 * it] = m_new
                st[2 * it + 1] = acc
            pipelined(work, past_scores(blocks_of), reduce)

        for qb in range(1, MOBA_QBLOCKS):
            attend(state, with_earlier(qb), earlier_in_step)

        def body(jj, st):
            st = list(st)
            attend(st, all_items, loop_blocks(jj))
            return tuple(st)

        state = lax.fori_loop(0, loop_trips, body, tuple(state))
        return tuple(state[1::2])

    def bounded_path():
        accs = [None] * n_items

        def own_reduce(it, raw):
            p = jnp.where(causal, jnp.exp(raw), 0.0).astype(BF16)
            accs[it] = _dot(v_rows(it, first + items[it][0]), p)

        pipelined(all_items, own_scores, own_reduce)

        def attend(st, work, blocks_of):
            def reduce(it, raws):
                acc = st[it]
                for j, s in zip(blocks_of(it), raws):
                    p = jnp.exp(s).astype(BF16)
                    acc = acc + _dot(v_rows(it, j), jnp.where(picked(it, j), p, jnp.zeros_like(p)))
                st[it] = acc
            pipelined(work, past_scores(blocks_of), reduce)

        for qb in range(1, MOBA_QBLOCKS):
            attend(accs, with_earlier(qb), earlier_in_step)

        def body(jj, st):
            st = list(st)
            attend(st, all_items, loop_blocks(jj))
            return tuple(st)

        return lax.fori_loop(0, loop_trips, body, tuple(accs))

    gq = gqk_ref[:, :MOBA_W]
    gk = gqk_ref[:, MOBA_W:]
    score_bound_sq = (HEAD_DIM * SCORE_BOUND_MARGIN) * jnp.max(gq * gq) * jnp.max(gk * gk)
    accs = lax.cond(score_bound_sq <= MOBA_SAFE_LOGIT * MOBA_SAFE_LOGIT, bounded_path, stable_path)
    for qb in range(MOBA_QBLOCKS):
        for pair in range(n_heads // 2):
            outs = []
            for h in (2 * pair, 2 * pair + 1):
                acc = accs[qb * n_heads + h]
                outs.append(acc[:HEAD_DIM] / acc[HEAD_DIM:HEAD_DIM + 1])
            o_ref[0, qb * tq:(qb + 1) * tq, pair * LANES:(pair + 1) * LANES] = (
                jnp.concatenate(outs, axis=0).T.astype(BF16))


def _moba_attention(q, k, vt, kmean, gqk):
    b, s, w = q.shape
    nb = s // MOBA_BLOCK
    rows = MOBA_QBLOCKS * MOBA_BLOCK
    return pl.pallas_call(
        _moba_kernel,
        grid=(b, nb // MOBA_QBLOCKS),
        in_specs=[
            pl.BlockSpec((1, rows, w), lambda bi, i: (bi, i, 0)),
            pl.BlockSpec((1, s, w), lambda bi, i: (bi, 0, 0)),
            pl.BlockSpec((1, w // HEAD_DIM, VT_ROWS, s), lambda bi, i: (bi, 0, 0, 0)),
            pl.BlockSpec((1, nb, w), lambda bi, i: (bi, 0, 0)),
            pl.BlockSpec((1, 2 * w), lambda bi, i: (0, 0)),
        ],
        out_specs=pl.BlockSpec((1, rows, w), lambda bi, i: (bi, i, 0)),
        out_shape=jax.ShapeDtypeStruct((b, s, w), BF16),
        compiler_params=pltpu.CompilerParams(dimension_semantics=("arbitrary",) * 2, vmem_limit_bytes=VMEM_LIMIT),
        name="moba_attention",
    )(q, k, vt, kmean, gqk)


def _memattn_kernel(q_ref, ck_ref, cv_ref, o_ref):
    rows = q_ref.shape[1]
    lane = lax.broadcasted_iota(jnp.int32, (rows, LANES), 1)
    low = lane < HEAD_DIM
    for pair in range(MEM_W // LANES):
        cols = slice(pair * LANES, (pair + 1) * LANES)
        q2 = q_ref[0, :, cols]
        kp = ck_ref[0, :, cols]
        vp = cv_ref[0, :, cols]
        zero = jnp.zeros_like(q2)
        outs = []
        for qm in (jnp.where(low, q2, zero), jnp.where(low, zero, q2)):
            s = _dot_nt(qm, kp)
            p = jnp.exp(s - jnp.max(s, axis=1, keepdims=True))
            outs.append(_dot(p.astype(BF16), vp) / jnp.sum(p, axis=1, keepdims=True))
        o_ref[0, :, cols] = jnp.where(low, outs[0], outs[1]).astype(BF16)


def _mem_attention(q, ck, cv):
    b, s, w = q.shape
    m = ck.shape[1]
    tm = MEMATTN_ROWS
    return pl.pallas_call(
        _memattn_kernel,
        grid=(b, s // tm),
        in_specs=[
            pl.BlockSpec((1, tm, w), lambda bi, i: (bi, i, 0)),
            pl.BlockSpec((1, m, w), lambda bi, i: (bi, 0, 0)),
            pl.BlockSpec((1, m, w), lambda bi, i: (bi, 0, 0)),
        ],
        out_specs=pl.BlockSpec((1, tm, w), lambda bi, i: (bi, i, 0)),
        out_shape=jax.ShapeDtypeStruct((b, s, w), BF16),
        compiler_params=pltpu.CompilerParams(dimension_semantics=("arbitrary", "arbitrary"),
                                             vmem_limit_bytes=VMEM_LIMIT),
        name="mem_attention",
    )(q, ck, cv)


def _sigmoid(t):
    return 1.0 / (1.0 + jnp.exp(-t))


def _tail_kernel(x_ref, osb_ref, omo_ref, ome_ref, gmix_ref, gffn_ref, wg_ref, wsb_ref, wmo_ref, wme_ref,
                 wout_ref, wfi_ref, wfd_ref, out_ref):
    x = x_ref[...]
    d = x.shape[1]
    h = _rms_rows(x, gmix_ref[...]).astype(BF16)
    mix = None
    for n, (o_ref, wu_ref) in enumerate(((osb_ref, wsb_ref), (omo_ref, wmo_ref), (ome_ref, wme_ref))):
        gate = _dot(h, wg_ref[:, n * d:(n + 1) * d])
        term = _sigmoid(gate) * _dot(o_ref[...], wu_ref[...])
        mix = term if mix is None else mix + term
    x1 = x + _dot(mix.astype(BF16), wout_ref[...])
    h2 = _rms_rows(x1, gffn_ref[...]).astype(BF16)
    dff = wfd_ref.shape[0]
    gate = _dot(h2, wfi_ref[:, :dff])
    up = _dot(h2, wfi_ref[:, dff:])
    ff = (gate * _sigmoid(gate) * up).astype(BF16)
    out_ref[...] = x1 + _dot(ff, wfd_ref[...])


def _tail(x2, osb, omo, ome, gmix, gffn, wg, wsb, wmo, wme, wout, wfi, wfd):
    n, d = x2.shape
    tm = TAIL_ROWS
    rows = lambda i: (i, 0)

    def resident(arr):
        return pl.BlockSpec(arr.shape, lambda i: (0, 0), pipeline_mode=pl.Buffered(1))

    return pl.pallas_call(
        _tail_kernel,
        grid=(n // tm,),
        in_specs=[
            pl.BlockSpec((tm, d), rows),
            pl.BlockSpec((tm, SB_W), rows),
            pl.BlockSpec((tm, MOBA_W), rows),
            pl.BlockSpec((tm, MEM_W), rows),
            resident(gmix), resident(gffn), resident(wg), resident(wsb), resident(wmo), resident(wme),
            resident(wout), resident(wfi), resident(wfd),
        ],
        out_specs=pl.BlockSpec((tm, d), rows),
        out_shape=jax.ShapeDtypeStruct((n, d), F32),
        compiler_params=pltpu.CompilerParams(dimension_semantics=("arbitrary",), vmem_limit_bytes=VMEM_LIMIT),
        name="merge_out_ffn",
    )(x2, osb, omo, ome, gmix, gffn, wg, wsb, wmo, wme, wout, wfi, wfd)


def _rope_tables(seq):
    half = HEAD_DIM // 2
    inv_freq = ROPE_THETA ** (-jnp.arange(half, dtype=F32) * 2.0 / HEAD_DIM)
    ang = jnp.arange(seq, dtype=F32)[:, None] * inv_freq[None, :]
    cos, sin = jnp.cos(ang), jnp.sin(ang)
    reps = LANES // HEAD_DIM
    return jnp.tile(jnp.concatenate([cos, cos], axis=1), (1, reps)), jnp.tile(jnp.concatenate([-sin, sin], axis=1), (1, reps))


def _head_mean_matrix(width):
    r = jnp.arange(width) // HEAD_DIM
    return jnp.where(r[:, None] == r[None, :], 1.0 / HEAD_DIM, 0.0).astype(BF16)


def _layer(x, mem, mix_norm_g, mem_norm_g, ffn_norm_g, w_in, w_mem_kv, moba_q_norm_g, moba_k_norm_g,
           mem_q_norm_g, mem_k_norm_g, w_up_sb, w_up_moba, w_up_mem, w_out, w_ffn_in, w_ffn_down):
    b, s, d = x.shape
    x2 = x.reshape(b * s, d)
    cos, sin = _rope_tables(s)
    bd = _head_mean_matrix(MXU_TILE)
    row = lambda g, reps=1: jnp.tile(g.astype(F32), reps)[None, :]
    gqk = jnp.concatenate([row(moba_q_norm_g, MOBA_HEADS), row(moba_k_norm_g, MOBA_HEADS)], axis=1)

    ck, cv = _memkv(mem, row(mem_norm_g), w_mem_kv.astype(BF16), bd, row(mem_k_norm_g, MEM_HEADS))
    sbq, sbk, sbv, moq, mok, movt, meq, kmean = _inproj(
        x2, row(mix_norm_g), w_in[:, :QKV_W].astype(BF16), cos, sin, bd, gqk, row(mem_q_norm_g, MEM_HEADS), s)
    seq3 = lambda t: t.reshape(b, s, t.shape[-1])
    o_sb = _sb_attention(seq3(sbq), seq3(sbk), seq3(sbv))
    o_mo = _moba_attention(seq3(moq), seq3(mok), movt, kmean.reshape(b, s // MOBA_BLOCK, MOBA_W), gqk)
    o_me = _mem_attention(seq3(meq), ck, cv)
    flat = lambda t: t.reshape(b * s, t.shape[-1])
    out = _tail(x2, flat(o_sb), flat(o_mo), flat(o_me), row(mix_norm_g), row(ffn_norm_g),
                w_in[:, QKV_W:].astype(BF16), w_up_sb.astype(BF16), w_up_moba.astype(BF16), w_up_mem.astype(BF16),
                w_out.astype(BF16), w_ffn_in.astype(BF16), w_ffn_down.astype(BF16))
    return out.reshape(b, s, d)


def kernel(x, mem, mix_norm_g, mem_norm_g, ffn_norm_g, w_in, w_mem_kv, moba_q_norm_g, moba_k_norm_g, mem_q_norm_g, mem_k_norm_g, w_up_sb, w_up_moba, w_up_mem, w_out, w_ffn_in, w_ffn_down):
    for l in range(w_in.shape[0]):
        x = _layer(x, mem, mix_norm_g[l], mem_norm_g[l], ffn_norm_g[l], w_in[l], w_mem_kv[l],
                   moba_q_norm_g[l], moba_k_norm_g[l], mem_q_norm_g[l], mem_k_norm_g[l],
                   w_up_sb[l], w_up_moba[l], w_up_mem[l], w_out[l], w_ffn_in[l], w_ffn_down[l])
    return x
```

```python
import functools
import math

import jax
import jax.numpy as jnp
from jax import lax
from jax.experimental import pallas as pl
from jax.experimental.pallas import tpu as pltpu

F32 = jnp.float32
BF16 = jnp.bfloat16

HEAD_DIM = 64
SB_HEADS = 6
MOBA_HEADS = 6
MEM_HEADS = 4
SB_W = SB_HEADS * HEAD_DIM
MOBA_W = MOBA_HEADS * HEAD_DIM
MEM_W = MEM_HEADS * HEAD_DIM
QKV_W = 3 * SB_W + 3 * MOBA_W + MEM_W
MOBA_BLOCK = 256
MOBA_TOPK = 3
ROPE_THETA = 10000.0
EPS = 1e-6
QK_SCALE = 1.0 / math.sqrt(HEAD_DIM)

LANES = 128
MXU_TILE = 256
BF16_SUBLANES = 16
VT_ROWS = HEAD_DIM + BF16_SUBLANES
NEG_BIG = -1e30

SB_DONE_LOG = -30.0

INPROJ_ROWS = 512
SB_ROWS = 256
SB_SUB = 64
SB_WIN = 128
SB_LOOKAHEAD = 10
MOBA_LOOKAHEAD = 2
MOBA_QBLOCKS = 4
MOBA_LOOP_BLOCKS = 4
MOBA_SAFE_LOGIT = 40.0
SCORE_BOUND_MARGIN = 1.1
MEMATTN_ROWS = 1024
MEMATTN_LOOKAHEAD = 5
TAIL_ROWS = 512
VMEM_LIMIT = 56 * 1024 * 1024


def _dot(a, b):
    return jnp.dot(a, b, preferred_element_type=F32)


def _dot_nt(a, b):
    return lax.dot_general(a, b, (((1,), (1,)), ((), ())), preferred_element_type=F32)


def _split_bf16(t):
    hi = t.astype(BF16)
    lo = (t - hi.astype(F32)).astype(BF16)
    return hi, lo


def _rms_rows(x, g):
    ms = jnp.mean(x * x, axis=-1, keepdims=True)
    return x * lax.rsqrt(ms + EPS) * g


def _head_rms(t, bd, g):
    hi, lo = _split_bf16(t * t)
    parts = []
    for c in range(0, t.shape[1], MXU_TILE):
        w = min(MXU_TILE, t.shape[1] - c)
        parts.append(_dot(hi[:, c:c + w], bd[:w, :w]) + _dot(lo[:, c:c + w], bd[:w, :w]))
    ms = parts[0] if len(parts) == 1 else jnp.concatenate(parts, axis=1)
    return t * lax.rsqrt(ms + EPS) * g


def _rotate_half_pairs(t):
    lane = lax.broadcasted_iota(jnp.int32, t.shape, 1)
    first_half = (lane & (HEAD_DIM - 1)) < (HEAD_DIM // 2)
    return jnp.where(first_half, pltpu.roll(t, LANES - HEAD_DIM // 2, 1), pltpu.roll(t, HEAD_DIM // 2, 1))


def _rope(t, cos, sin_signed):
    outs = []
    for c in range(t.shape[1] // LANES):
        tc = t[:, c * LANES:(c + 1) * LANES]
        outs.append(tc * cos + _rotate_half_pairs(tc) * sin_signed)
    return jnp.concatenate(outs, axis=1)


def _memkv_kernel(mem_ref, g_ref, w_ref, bd_ref, gk_ref, ck_ref, cvt_ref):
    h = _rms_rows(mem_ref[0], g_ref[...]).astype(BF16)
    kv = _dot(h, w_ref[...])
    k = kv[:, :MEM_W]
    ck_ref[0] = _head_rms(k, bd_ref[...], gk_ref[...]).astype(BF16)
    vt = kv[:, MEM_W:].T
    for hd in range(MEM_HEADS):
        cvt_ref[0, hd, 0:HEAD_DIM, :] = vt[hd * HEAD_DIM:(hd + 1) * HEAD_DIM, :].astype(BF16)
        cvt_ref[0, hd, HEAD_DIM:, :] = jnp.ones((VT_ROWS - HEAD_DIM, vt.shape[1]), BF16)


def _memkv(mem, g, w, bd, gk):
    b, m, d = mem.shape
    return pl.pallas_call(
        _memkv_kernel,
        grid=(b,),
        in_specs=[
            pl.BlockSpec((1, m, d), lambda i: (i, 0, 0)),
            pl.BlockSpec((1, d), lambda i: (0, 0)),
            pl.BlockSpec((d, 2 * MEM_W), lambda i: (0, 0)),
            pl.BlockSpec((MXU_TILE, MXU_TILE), lambda i: (0, 0)),
            pl.BlockSpec((1, MEM_W), lambda i: (0, 0)),
        ],
        out_specs=[
            pl.BlockSpec((1, m, MEM_W), lambda i: (i, 0, 0)),
            pl.BlockSpec((1, MEM_HEADS, VT_ROWS, m), lambda i: (i, 0, 0, 0)),
        ],
        out_shape=[jax.ShapeDtypeStruct((b, m, MEM_W), BF16),
                   jax.ShapeDtypeStruct((b, MEM_HEADS, VT_ROWS, m), BF16)],
        compiler_params=pltpu.CompilerParams(dimension_semantics=("arbitrary",), vmem_limit_bytes=VMEM_LIMIT),
        name="memkv",
    )(mem, g, w, bd, gk)


def _inproj_kernel(x_ref, g_ref, wa_ref, wb_ref, wc_ref, cos_ref, sin_ref, bd_ref, gqk_ref, gmq_ref,
                   sbq_ref, sbk_ref, sbv_ref, moq_ref, mok_ref, movt_ref, meq_ref, kmean_ref):
    h = _rms_rows(x_ref[...], g_ref[...]).astype(BF16)
    pa = _dot(h, wa_ref[...])
    pb = _dot(h, wb_ref[...])
    bd = bd_ref[...]
    qk_normed = _head_rms(pa[:, :2 * MOBA_W], bd, gqk_ref[...])
    cq = _head_rms(pa[:, 3 * MOBA_W:3 * MOBA_W + MEM_W], bd, gmq_ref[...])
    pc = _dot(h, wc_ref[...])

    qk = _rope(qk_normed, cos_ref[...], sin_ref[...])
    moq_ref[...] = (qk[:, :MOBA_W] * QK_SCALE).astype(BF16)
    mk = qk[:, MOBA_W:]
    mok_ref[...] = mk.astype(BF16)
    for blk in range(INPROJ_ROWS // MOBA_BLOCK):
        kmean_ref[0, blk:blk + 1, :] = jnp.mean(mk[blk * MOBA_BLOCK:(blk + 1) * MOBA_BLOCK], axis=0, keepdims=True)
    vt = pa[:, 2 * MOBA_W:3 * MOBA_W].T
    for hd in range(MOBA_HEADS):
        movt_ref[0, hd, 0:HEAD_DIM, :] = vt[hd * HEAD_DIM:(hd + 1) * HEAD_DIM, :].astype(BF16)
        movt_ref[0, hd, HEAD_DIM:, :] = jnp.ones((VT_ROWS - HEAD_DIM, vt.shape[1]), BF16)
    meq_ref[...] = (cq * QK_SCALE).astype(BF16)

    sb = jnp.concatenate([pa[:, 3 * MOBA_W + MEM_W:], pb, pc], axis=1)
    sbq_ref[...] = (sb[:, :SB_W] * QK_SCALE).astype(BF16)
    sbk_ref[...] = sb[:, SB_W:2 * SB_W].astype(BF16)
    sbv_ref[...] = sb[:, 2 * SB_W:].astype(BF16)


def _split_projection(w):
    sb = w[:, :3 * SB_W]
    rest = w[:, 3 * SB_W:]
    head = -rest.shape[1] % MXU_TILE
    mid = head + (sb.shape[1] - head) // (2 * MXU_TILE) * MXU_TILE
    return jnp.concatenate([rest, sb[:, :head]], axis=1), sb[:, head:mid], sb[:, mid:]


def _inproj(x2, g, ws, cos, sin, bd, gqk, gmq, seq):
    n, d = x2.shape
    tm = INPROJ_ROWS
    steps = n // tm
    per_seq = seq // tm
    blocks_per_step = tm // MOBA_BLOCK
    const = lambda i: (0, 0)
    rows = lambda i: (i, 0)
    out_shapes = [jax.ShapeDtypeStruct((n, SB_W), BF16)] * 3 + [jax.ShapeDtypeStruct((n, MOBA_W), BF16)] * 2 + [
        jax.ShapeDtypeStruct((n // seq, MOBA_HEADS, VT_ROWS, seq), BF16),
        jax.ShapeDtypeStruct((n, MEM_W), BF16),
        jax.ShapeDtypeStruct((steps, blocks_per_step, MOBA_W), F32),
    ]
    out_specs = [pl.BlockSpec((tm, SB_W), rows)] * 3 + [pl.BlockSpec((tm, MOBA_W), rows)] * 2 + [
        pl.BlockSpec((1, MOBA_HEADS, VT_ROWS, tm), lambda i: (i // per_seq, 0, 0, i % per_seq)),
        pl.BlockSpec((tm, MEM_W), rows),
        pl.BlockSpec((1, blocks_per_step, MOBA_W), lambda i: (i, 0, 0)),
    ]
    return pl.pallas_call(
        _inproj_kernel,
        grid=(steps,),
        in_specs=[
            pl.BlockSpec((tm, d), rows),
            pl.BlockSpec((1, d), const),
            *[pl.BlockSpec(w.shape, const) for w in ws],
            pl.BlockSpec((tm, LANES), lambda i: (i % per_seq, 0)),
            pl.BlockSpec((tm, LANES), lambda i: (i % per_seq, 0)),
            pl.BlockSpec((MXU_TILE, MXU_TILE), const),
            pl.BlockSpec((1, 2 * MOBA_W), const),
            pl.BlockSpec((1, MEM_W), const),
        ],
        out_specs=out_specs,
        out_shape=out_shapes,
        compiler_params=pltpu.CompilerParams(dimension_semantics=("arbitrary",), vmem_limit_bytes=VMEM_LIMIT),
        name="inproj",
    )(x2, g, *ws, cos, sin, bd, gqk, gmq)


def _sb_kernel(q_ref, k_ref, v_ref, o_ref):
    i = pl.program_id(1)
    lane = lax.broadcasted_iota(jnp.int32, (SB_SUB, LANES), 1)
    low = lane < HEAD_DIM
    kiota = lax.broadcasted_iota(jnp.int32, (SB_SUB, SB_WIN), 1)
    riota = lax.broadcasted_iota(jnp.int32, (SB_SUB, 1), 0)
    ur = lax.broadcasted_iota(jnp.int32, (SB_WIN, SB_WIN), 0)
    uc = lax.broadcasted_iota(jnp.int32, (SB_WIN, SB_WIN), 1)
    later = (ur > uc).astype(BF16)
    n_pairs = SB_W // LANES

    n_sub = SB_ROWS // SB_SUB
    chains = []
    for sub in range(n_sub):
        for pair in range(n_pairs):
            q2 = q_ref[0, sub * SB_SUB:(sub + 1) * SB_SUB, pair * LANES:(pair + 1) * LANES]
            zero = jnp.zeros_like(q2)
            chains += [(sub, pair, jnp.where(low, q2, zero)), (sub, pair, jnp.where(low, zero, q2))]
    n_chains = len(chains)

    def scan_windows(starts, limits, carries):
        def cols(c):
            return slice(chains[c][1] * LANES, (chains[c][1] + 1) * LANES)

        def scores(c):
            sub = chains[c][0]
            return _dot_nt(chains[c][2], k_ref[0, pl.ds(starts[sub], SB_WIN), cols(c)])

        valids = [(starts[sub] + kiota) < limits[sub] for sub in range(n_sub)]

        def scan(c, z):
            valid = valids[chains[c][0]]
            log_beta = jnp.minimum(z, 0.0) - jnp.log(1.0 + jnp.exp(-jnp.abs(z)))
            log_1m = jnp.where(valid, log_beta - z, 0.0)
            hi, lo = _split_bf16(log_1m)
            tail = _dot(hi, later) + _dot(lo, later)
            return log_beta, tail, carries[c] + jnp.sum(log_1m, axis=1, keepdims=True)

        def weigh(c, log_beta, tail):
            sub = chains[c][0]
            a = jnp.where(valids[sub], jnp.exp(log_beta + tail + carries[c]), 0.0)
            return _dot(a.astype(BF16), v_ref[0, pl.ds(starts[sub], SB_WIN), cols(c)])

        zs, scans, outs = {}, {}, [None] * n_chains
        for t in range(n_chains + 2 * SB_LOOKAHEAD):
            if t < n_chains:
                zs[t] = scores(t)
            c = t - SB_LOOKAHEAD
            if 0 <= c < n_chains:
                scans[c] = scan(c, zs.pop(c))
            c = t - 2 * SB_LOOKAHEAD
            if 0 <= c < n_chains:
                log_beta, tail, carry = scans.pop(c)
                outs[c] = (weigh(c, log_beta, tail), carry)
        return outs

    def unfinished(cs):
        worst = cs[0]
        for c in cs[1:]:
            worst = jnp.maximum(worst, c)
        return (jnp.max(worst) > SB_DONE_LOG).astype(jnp.int32)

    t0s = [i * SB_ROWS + sub * SB_SUB for sub in range(n_sub)]
    zero_carries = [jnp.zeros((SB_SUB, 1), F32)] * n_chains

    def window_before(ends):
        starts = [pl.multiple_of(jnp.maximum(e - SB_WIN, 0), SB_SUB) for e in ends]
        limits = [jnp.minimum(t0 + riota, e) for t0, e in zip(t0s, ends)]
        return starts, limits

    def write(accs):
        for sub in range(n_sub):
            for pair in range(n_pairs):
                c = (sub * n_pairs + pair) * 2
                out = jnp.where(low, accs[c], accs[c + 1])
                o_ref[0, sub * SB_SUB:(sub + 1) * SB_SUB, pair * LANES:(pair + 1) * LANES] = out.astype(BF16)

    ends0 = [t0 + SB_SUB for t0 in t0s]
    starts0, limits0 = window_before(ends0)
    first = scan_windows(starts0, limits0, zero_carries)
    write([acc for acc, _ in first])
    last_start = starts0[-1]

    @pl.when(jnp.logical_and(unfinished([carry for _, carry in first]) > 0, last_start > 0))
    def _():
        def cond(state):
            ends, more = state[:n_sub], state[n_sub]
            return jnp.logical_and(ends[-1] > 0, more > 0)

        def body(state):
            ends = state[:n_sub]
            accs_c = state[n_sub + 1:n_sub + 1 + n_chains]
            cs_c = state[n_sub + 1 + n_chains:]
            starts, limits = window_before(ends)
            res = scan_windows(starts, limits, list(cs_c))
            new_cs = [r[1] for r in res]
            return (*starts, unfinished(new_cs), *[a + r[0] for a, r in zip(accs_c, res)], *new_cs)

        zero_accs = [jnp.zeros((SB_SUB, LANES), F32)] * n_chains
        state = lax.while_loop(cond, body, (*ends0, jnp.int32(1), *zero_accs, *zero_carries))
        write(state[n_sub + 1:n_sub + 1 + n_chains])


def _sb_attention(q, k, v):
    b, s, w = q.shape
    return pl.pallas_call(
        _sb_kernel,
        grid=(b, s // SB_ROWS),
        in_specs=[
            pl.BlockSpec((1, SB_ROWS, w), lambda bi, i: (bi, i, 0)),
            pl.BlockSpec((1, s, w), lambda bi, i: (bi, 0, 0)),
            pl.BlockSpec((1, s, w), lambda bi, i: (bi, 0, 0)),
        ],
        out_specs=pl.BlockSpec((1, SB_ROWS, w), lambda bi, i: (bi, i, 0)),
        out_shape=jax.ShapeDtypeStruct((b, s, w), BF16),
        compiler_params=pltpu.CompilerParams(dimension_semantics=("arbitrary", "arbitrary"),
                                             vmem_limit_bytes=VMEM_LIMIT),
        name="sb_attention",
    )(q, k, v)


def _moba_kernel(q_ref, k_ref, vt_ref, km_ref, gqk_ref, o_ref):
    first = pl.program_id(1) * MOBA_QBLOCKS
    nb = km_ref.shape[1]
    tq = MOBA_BLOCK
    n_heads = vt_ref.shape[1]
    lane = lax.broadcasted_iota(jnp.int32, (tq, LANES), 1)
    low = lane < HEAD_DIM
    blk = lax.broadcasted_iota(jnp.int32, (nb, tq), 0)
    items = [(qb, h) for qb in range(MOBA_QBLOCKS) for h in range(n_heads)]

    def cols(h):
        pair = h // 2
        return slice(pair * LANES, (pair + 1) * LANES)

    qms = []
    for qb in range(MOBA_QBLOCKS):
        for pair in range(n_heads // 2):
            q2 = q_ref[0, qb * tq:(qb + 1) * tq, pair * LANES:(pair + 1) * LANES]
            zero = jnp.zeros_like(q2)
            qms += [jnp.where(low, q2, zero), jnp.where(low, zero, q2)]

    bits = [None] * len(items)

    def select_blocks():
        km_split = [_split_bf16(km_ref[0, :, cols(h)]) for h in range(n_heads)]
        gates = [_dot_nt(km_split[h][0], qms[it]) + _dot_nt(km_split[h][1], qms[it])
                 for it, (_, h) in enumerate(items)]
        blkf = blk.astype(F32)
        for it, (qb, _) in enumerate(items):
            gate = jnp.where(blk < first + qb, gates[it], -jnp.inf)
            chosen = jnp.zeros((1, tq), jnp.int32)
            for _ in range(MOBA_TOPK):
                best = jnp.max(gate, axis=0, keepdims=True)
                idx = jnp.min(jnp.where(gate == best, blkf, float(nb)), axis=0, keepdims=True)
                chosen = chosen | jnp.where(best > -jnp.inf, jnp.left_shift(1, idx.astype(jnp.int32)), 0)
                gate = jnp.where(blkf == idx, -jnp.inf, gate)
            bits[it] = chosen

    def block_offset(j):
        return pl.multiple_of(j * tq, tq)

    def pipelined(work, scores, reduce):
        queued = {w: scores(w) for w in work[:MOBA_LOOKAHEAD]}
        for n, w in enumerate(work):
            if n + MOBA_LOOKAHEAD < len(work):
                ahead = work[n + MOBA_LOOKAHEAD]
                queued[ahead] = scores(ahead)
            reduce(w, queued.pop(w))

    causal = lax.broadcasted_iota(jnp.int32, (tq, tq), 0) <= lax.broadcasted_iota(jnp.int32, (tq, tq), 1)
    n_items = len(items)
    all_items = list(range(n_items))

    def own_scores(it):
        qb, h = items[it]
        return _dot_nt(k_ref[0, pl.ds(block_offset(first + qb), tq), cols(h)], qms[it])

    def past_scores(blocks_of):
        def scores(it):
            h = items[it][1]
            return [_dot_nt(k_ref[0, pl.ds(block_offset(j), tq), cols(h)], qms[it]) for j in blocks_of(it)]
        return scores

    def v_rows(it, j):
        return vt_ref[0, items[it][1], :, pl.ds(block_offset(j), tq)]

    def picked(it, j):
        return (jnp.right_shift(bits[it], j) & 1) > 0

    def earlier_in_step(it):
        return [first + e for e in range(items[it][0])]

    def loop_blocks(jj):
        return lambda it: [MOBA_LOOP_BLOCKS * jj + u for u in range(MOBA_LOOP_BLOCKS)]

    loop_trips = first // MOBA_LOOP_BLOCKS

    def with_earlier(qb):
        return [it for it in all_items if items[it][0] == qb]

    def stable_path():
        state = [None] * (2 * n_items)
        select_blocks()

        def own_reduce(it, raw):
            s = jnp.where(causal, raw, NEG_BIG)
            m = jnp.max(s, axis=0, keepdims=True)
            state[2 * it] = m
            state[2 * it + 1] = _dot(v_rows(it, first + items[it][0]), jnp.exp(s - m).astype(BF16))

        pipelined(all_items, own_scores, own_reduce)

        def attend(st, work, blocks_of):
            def reduce(it, raws):
                m, acc = st[2 * it], st[2 * it + 1]
                picks = [picked(it, j) for j in blocks_of(it)]
                m_new = m
                for pick, s in zip(picks, raws):
                    m_new = jnp.maximum(m_new, jnp.where(pick, jnp.max(s, axis=0, keepdims=True), NEG_BIG))
                acc = jnp.exp(m - m_new) * acc
                for j, pick, s in zip(blocks_of(it), picks, raws):
                    p = jnp.exp((s - m_new).astype(BF16))
                    acc = acc + _dot(v_rows(it, j), jnp.where(pick, p, jnp.zeros_like(p)))
                st[2 * it] = m_new
                st[2 * it + 1] = acc
            pipelined(work, past_scores(blocks_of), reduce)

        for qb in range(1, MOBA_QBLOCKS):
            attend(state, with_earlier(qb), earlier_in_step)

        def body(jj, st):
            st = list(st)
            attend(st, all_items, loop_blocks(jj))
            return tuple(st)

        state = lax.fori_loop(0, loop_trips, body, tuple(state))
        return tuple(state[1::2])

    def bounded_path():
        accs = [None] * n_items

        def weighted(it, j, s, visible):
            p = jnp.exp(s).astype(BF16)
            return _dot(v_rows(it, j), jnp.where(visible, p, jnp.zeros_like(p)))

        def step_blocks(it):
            return earlier_in_step(it) + [first + items[it][0]]

        def step_reduce(it, raws):
            blocks = step_blocks(it)
            terms = [weighted(it, j, s, picked(it, j)) for j, s in zip(blocks[:-1], raws[:-1])]
            terms.append(weighted(it, blocks[-1], raws[-1], causal))
            accs[it] = functools.reduce(lambda a, b: a + b, terms)

        select_blocks()
        pipelined(all_items, past_scores(step_blocks), step_reduce)

        def attend(st, work, blocks_of):
            def reduce(it, raws):
                acc = st[it]
                for j, s in zip(blocks_of(it), raws):
                    acc = acc + weighted(it, j, s, picked(it, j))
                st[it] = acc
            pipelined(work, past_scores(blocks_of), reduce)

        def body(jj, st):
            st = list(st)
            attend(st, all_items, loop_blocks(jj))
            return tuple(st)

        return lax.fori_loop(0, loop_trips, body, tuple(accs))

    gq = gqk_ref[:, :MOBA_W]
    gk = gqk_ref[:, MOBA_W:]
    score_bound_sq = (HEAD_DIM * SCORE_BOUND_MARGIN) * jnp.max(gq * gq) * jnp.max(gk * gk)
    accs = lax.cond(score_bound_sq <= MOBA_SAFE_LOGIT * MOBA_SAFE_LOGIT, bounded_path, stable_path)
    for qb in range(MOBA_QBLOCKS):
        for pair in range(n_heads // 2):
            outs = []
            for h in (2 * pair, 2 * pair + 1):
                acc = accs[qb * n_heads + h]
                outs.append(acc[:HEAD_DIM] / acc[HEAD_DIM:HEAD_DIM + 1])
            o_ref[0, qb * tq:(qb + 1) * tq, pair * LANES:(pair + 1) * LANES] = (
                jnp.concatenate(outs, axis=0).T.astype(BF16))


def _moba_attention(q, k, vt, kmean, gqk):
    b, s, w = q.shape
    nb = s // MOBA_BLOCK
    rows = MOBA_QBLOCKS * MOBA_BLOCK
    return pl.pallas_call(
        _moba_kernel,
        grid=(b, nb // MOBA_QBLOCKS),
        in_specs=[
            pl.BlockSpec((1, rows, w), lambda bi, i: (bi, i, 0)),
            pl.BlockSpec((1, s, w), lambda bi, i: (bi, 0, 0)),
            pl.BlockSpec((1, w // HEAD_DIM, VT_ROWS, s), lambda bi, i: (bi, 0, 0, 0)),
            pl.BlockSpec((1, nb, w), lambda bi, i: (bi, 0, 0)),
            pl.BlockSpec((1, 2 * w), lambda bi, i: (0, 0)),
        ],
        out_specs=pl.BlockSpec((1, rows, w), lambda bi, i: (bi, i, 0)),
        out_shape=jax.ShapeDtypeStruct((b, s, w), BF16),
        compiler_params=pltpu.CompilerParams(dimension_semantics=("arbitrary",) * 2, vmem_limit_bytes=VMEM_LIMIT),
        name="moba_attention",
    )(q, k, vt, kmean, gqk)


def _memattn_kernel(q_ref, ck_ref, cvt_ref, o_ref):
    tq = MXU_TILE
    lane = lax.broadcasted_iota(jnp.int32, (tq, LANES), 1)
    low = lane < HEAD_DIM
    work = [(tile, hd) for tile in range(q_ref.shape[1] // tq) for hd in range(MEM_HEADS)]

    def cols(hd):
        return slice((hd // 2) * LANES, (hd // 2 + 1) * LANES)

    def scores(w):
        tile, hd = work[w]
        q2 = q_ref[0, tile * tq:(tile + 1) * tq, cols(hd)]
        zero = jnp.zeros_like(q2)
        qm = jnp.where(low, q2, zero) if hd % 2 == 0 else jnp.where(low, zero, q2)
        return _dot_nt(ck_ref[0, :, cols(hd)], qm)

    queued = {w: scores(w) for w in range(MEMATTN_LOOKAHEAD)}
    outs = {}
    for w, (tile, hd) in enumerate(work):
        if w + MEMATTN_LOOKAHEAD < len(work):
            queued[w + MEMATTN_LOOKAHEAD] = scores(w + MEMATTN_LOOKAHEAD)
        s = queued.pop(w)
        p = jnp.exp(s - jnp.max(s, axis=0, keepdims=True)).astype(BF16)
        acc = _dot(cvt_ref[0, hd], p)
        outs[hd % 2] = acc[:HEAD_DIM] / acc[HEAD_DIM:HEAD_DIM + 1]
        if hd % 2 == 1:
            o_ref[0, tile * tq:(tile + 1) * tq, cols(hd)] = (
                jnp.concatenate([outs[0], outs[1]], axis=0).T.astype(BF16))


def _mem_attention(q, ck, cv):
    b, s, w = q.shape
    m = ck.shape[1]
    tm = MEMATTN_ROWS
    return pl.pallas_call(
        _memattn_kernel,
        grid=(b, s // tm),
        in_specs=[
            pl.BlockSpec((1, tm, w), lambda bi, i: (bi, i, 0)),
            pl.BlockSpec((1, m, w), lambda bi, i: (bi, 0, 0)),
            pl.BlockSpec((1, MEM_HEADS, VT_ROWS, m), lambda bi, i: (bi, 0, 0, 0)),
        ],
        out_specs=pl.BlockSpec((1, tm, w), lambda bi, i: (bi, i, 0)),
        out_shape=jax.ShapeDtypeStruct((b, s, w), BF16),
        compiler_params=pltpu.CompilerParams(dimension_semantics=("arbitrary", "arbitrary"),
                                             vmem_limit_bytes=VMEM_LIMIT),
        name="mem_attention",
    )(q, ck, cv)


def _sigmoid(t):
    return 1.0 / (1.0 + jnp.exp(-t))


def _tail_kernel(x_ref, osb_ref, omo_ref, ome_ref, gmix_ref, gffn_ref, wg_ref, wsb_ref, wmo_ref, wme_ref,
                 wout_ref, wfi_ref, wfd_ref, out_ref):
    x = x_ref[...]
    d = x.shape[1]
    h = _rms_rows(x, gmix_ref[...]).astype(BF16)
    mix = None
    for n, (o_ref, wu_ref) in enumerate(((osb_ref, wsb_ref), (omo_ref, wmo_ref), (ome_ref, wme_ref))):
        gate = _dot(h, wg_ref[:, n * d:(n + 1) * d])
        term = _sigmoid(gate) * _dot(o_ref[...], wu_ref[...])
        mix = term if mix is None else mix + term
    x1 = x + _dot(mix.astype(BF16), wout_ref[...])
    h2 = _rms_rows(x1, gffn_ref[...]).astype(BF16)
    dff = wfd_ref.shape[0]
    gate = _dot(h2, wfi_ref[:, :dff])
    up = _dot(h2, wfi_ref[:, dff:])
    ff = (gate * _sigmoid(gate) * up).astype(BF16)
    out_ref[...] = x1 + _dot(ff, wfd_ref[...])


def _tail(x2, osb, omo, ome, gmix, gffn, wg, wsb, wmo, wme, wout, wfi, wfd):
    n, d = x2.shape
    tm = TAIL_ROWS
    rows = lambda i: (i, 0)

    def resident(arr):
        return pl.BlockSpec(arr.shape, lambda i: (0, 0), pipeline_mode=pl.Buffered(1))

    return pl.pallas_call(
        _tail_kernel,
        grid=(n // tm,),
        in_specs=[
            pl.BlockSpec((tm, d), rows),
            pl.BlockSpec((tm, SB_W), rows),
            pl.BlockSpec((tm, MOBA_W), rows),
            pl.BlockSpec((tm, MEM_W), rows),
            resident(gmix), resident(gffn), resident(wg), resident(wsb), resident(wmo), resident(wme),
            resident(wout), resident(wfi), resident(wfd),
        ],
        out_specs=pl.BlockSpec((tm, d), rows),
        out_shape=jax.ShapeDtypeStruct((n, d), F32),
        compiler_params=pltpu.CompilerParams(dimension_semantics=("arbitrary",), vmem_limit_bytes=VMEM_LIMIT),
        name="merge_out_ffn",
    )(x2, osb, omo, ome, gmix, gffn, wg, wsb, wmo, wme, wout, wfi, wfd)


def _rope_tables(seq):
    half = HEAD_DIM // 2
    inv_freq = ROPE_THETA ** (-jnp.arange(half, dtype=F32) * 2.0 / HEAD_DIM)
    ang = jnp.arange(seq, dtype=F32)[:, None] * inv_freq[None, :]
    cos, sin = jnp.cos(ang), jnp.sin(ang)
    reps = LANES // HEAD_DIM
    return jnp.tile(jnp.concatenate([cos, cos], axis=1), (1, reps)), jnp.tile(jnp.concatenate([-sin, sin], axis=1), (1, reps))


def _head_mean_matrix(width):
    r = jnp.arange(width) // HEAD_DIM
    return jnp.where(r[:, None] == r[None, :], 1.0 / HEAD_DIM, 0.0).astype(BF16)


def _layer(x, mem, mix_norm_g, mem_norm_g, ffn_norm_g, w_in, w_mem_kv, moba_q_norm_g, moba_k_norm_g,
           mem_q_norm_g, mem_k_norm_g, w_up_sb, w_up_moba, w_up_mem, w_out, w_ffn_in, w_ffn_down):
    b, s, d = x.shape
    x2 = x.reshape(b * s, d)
    cos, sin = _rope_tables(s)
    bd = _head_mean_matrix(MXU_TILE)
    row = lambda g, reps=1: jnp.tile(g.astype(F32), reps)[None, :]
    gqk = jnp.concatenate([row(moba_q_norm_g, MOBA_HEADS), row(moba_k_norm_g, MOBA_HEADS)], axis=1)

    ck, cv = _memkv(mem, row(mem_norm_g), w_mem_kv.astype(BF16), bd, row(mem_k_norm_g, MEM_HEADS))
    sbq, sbk, sbv, moq, mok, movt, meq, kmean = _inproj(
        x2, row(mix_norm_g), _split_projection(w_in[:, :QKV_W].astype(BF16)), cos, sin, bd, gqk,
        row(mem_q_norm_g, MEM_HEADS), s)
    seq3 = lambda t: t.reshape(b, s, t.shape[-1])
    o_sb = _sb_attention(seq3(sbq), seq3(sbk), seq3(sbv))
    o_mo = _moba_attention(seq3(moq), seq3(mok), movt, kmean.reshape(b, s // MOBA_BLOCK, MOBA_W), gqk)
    o_me = _mem_attention(seq3(meq), ck, cv)
    flat = lambda t: t.reshape(b * s, t.shape[-1])
    out = _tail(x2, flat(o_sb), flat(o_mo), flat(o_me), row(mix_norm_g), row(ffn_norm_g),
                w_in[:, QKV_W:].astype(BF16), w_up_sb.astype(BF16), w_up_moba.astype(BF16), w_up_mem.astype(BF16),
                w_out.astype(BF16), w_ffn_in.astype(BF16), w_ffn_down.astype(BF16))
    return out.reshape(b, s, d)


def kernel(x, mem, mix_norm_g, mem_norm_g, ffn_norm_g, w_in, w_mem_kv, moba_q_norm_g, moba_k_norm_g, mem_q_norm_g, mem_k_norm_g, w_up_sb, w_up_moba, w_up_mem, w_out, w_ffn_in, w_ffn_down):
    for l in range(w_in.shape[0]):
        x = _layer(x, mem, mix_norm_g[l], mem_norm_g[l], ffn_norm_g[l], w_in[l], w_mem_kv[l],
                   moba_q_norm_g[l], moba_k_norm_g[l], mem_q_norm_g[l], mem_k_norm_g[l],
                   w_up_sb[l], w_up_moba[l], w_up_mem[l], w_out[l], w_ffn_in[l], w_ffn_down[l])
    return x
```

```python
import functools
import math

import jax
import jax.numpy as jnp
from jax import lax
from jax.experimental import pallas as pl
from jax.experimental.pallas import tpu as pltpu

F32 = jnp.float32
BF16 = jnp.bfloat16

HEAD_DIM = 64
SB_HEADS = 6
MOBA_HEADS = 6
MEM_HEADS = 4
SB_W = SB_HEADS * HEAD_DIM
MOBA_W = MOBA_HEADS * HEAD_DIM
MEM_W = MEM_HEADS * HEAD_DIM
QKV_W = 3 * SB_W + 3 * MOBA_W + MEM_W
MOBA_BLOCK = 256
MOBA_TOPK = 3
ROPE_THETA = 10000.0
EPS = 1e-6
QK_SCALE = 1.0 / math.sqrt(HEAD_DIM)

LANES = 128
MXU_TILE = 256
BF16_SUBLANES = 16
VT_ROWS = HEAD_DIM + BF16_SUBLANES
NEG_BIG = -1e30

SB_DONE_LOG = -30.0

INPROJ_ROWS = 512
SB_ROWS = 512
SB_SUB = 64
SB_WIN = 128
SB_LOOKAHEAD = 10
MOBA_LOOKAHEAD = 2
MOBA_QBLOCKS = 4
MOBA_LOOP_BLOCKS = 4
MOBA_ENTRY_BLOCKS = 2
MOBA_ENTRY_LOOKAHEAD = 3
MOBA_SAFE_LOGIT = 40.0
SCORE_BOUND_MARGIN = 1.1
MEMATTN_ROWS = 1024
MEMATTN_LOOKAHEAD = 5
TAIL_ROWS = 512
VMEM_LIMIT = 56 * 1024 * 1024


def _dot(a, b):
    return jnp.dot(a, b, preferred_element_type=F32)


def _dot_nt(a, b):
    return lax.dot_general(a, b, (((1,), (1,)), ((), ())), preferred_element_type=F32)


def _split_bf16(t):
    hi = t.astype(BF16)
    lo = (t - hi.astype(F32)).astype(BF16)
    return hi, lo


def _rms_rows(x, g):
    ms = jnp.mean(x * x, axis=-1, keepdims=True)
    return x * lax.rsqrt(ms + EPS) * g


def _head_rms(t, bd, g):
    hi, lo = _split_bf16(t * t)
    parts = []
    for c in range(0, t.shape[1], MXU_TILE):
        w = min(MXU_TILE, t.shape[1] - c)
        parts.append(_dot(hi[:, c:c + w], bd[:w, :w]) + _dot(lo[:, c:c + w], bd[:w, :w]))
    ms = parts[0] if len(parts) == 1 else jnp.concatenate(parts, axis=1)
    return t * lax.rsqrt(ms + EPS) * g


def _rotate_half_pairs(t):
    lane = lax.broadcasted_iota(jnp.int32, t.shape, 1)
    first_half = (lane & (HEAD_DIM - 1)) < (HEAD_DIM // 2)
    return jnp.where(first_half, pltpu.roll(t, LANES - HEAD_DIM // 2, 1), pltpu.roll(t, HEAD_DIM // 2, 1))


def _rope(t, cos, sin_signed):
    outs = []
    for c in range(t.shape[1] // LANES):
        tc = t[:, c * LANES:(c + 1) * LANES]
        outs.append(tc * cos + _rotate_half_pairs(tc) * sin_signed)
    return jnp.concatenate(outs, axis=1)


def _memkv_kernel(mem_ref, g_ref, w_ref, bd_ref, gk_ref, ck_ref, cvt_ref):
    h = _rms_rows(mem_ref[0], g_ref[...]).astype(BF16)
    kv = _dot(h, w_ref[...])
    k = kv[:, :MEM_W]
    ck_ref[0] = _head_rms(k, bd_ref[...], gk_ref[...]).astype(BF16)
    vt = kv[:, MEM_W:].T
    for hd in range(MEM_HEADS):
        cvt_ref[0, hd, 0:HEAD_DIM, :] = vt[hd * HEAD_DIM:(hd + 1) * HEAD_DIM, :].astype(BF16)
        cvt_ref[0, hd, HEAD_DIM:, :] = jnp.ones((VT_ROWS - HEAD_DIM, vt.shape[1]), BF16)


def _memkv(mem, g, w, bd, gk):
    b, m, d = mem.shape
    return pl.pallas_call(
        _memkv_kernel,
        grid=(b,),
        in_specs=[
            pl.BlockSpec((1, m, d), lambda i: (i, 0, 0)),
            pl.BlockSpec((1, d), lambda i: (0, 0)),
            pl.BlockSpec((d, 2 * MEM_W), lambda i: (0, 0)),
            pl.BlockSpec((MXU_TILE, MXU_TILE), lambda i: (0, 0)),
            pl.BlockSpec((1, MEM_W), lambda i: (0, 0)),
        ],
        out_specs=[
            pl.BlockSpec((1, m, MEM_W), lambda i: (i, 0, 0)),
            pl.BlockSpec((1, MEM_HEADS, VT_ROWS, m), lambda i: (i, 0, 0, 0)),
        ],
        out_shape=[jax.ShapeDtypeStruct((b, m, MEM_W), BF16),
                   jax.ShapeDtypeStruct((b, MEM_HEADS, VT_ROWS, m), BF16)],
        compiler_params=pltpu.CompilerParams(dimension_semantics=("arbitrary",), vmem_limit_bytes=VMEM_LIMIT),
        name="memkv",
    )(mem, g, w, bd, gk)


def _inproj_kernel(x_ref, g_ref, wa_ref, wb_ref, wc_ref, cos_ref, sin_ref, bd_ref, gqk_ref, gmq_ref,
                   sbq_ref, sbk_ref, sbv_ref, moqt_ref, mok_ref, movt_ref, meq_ref, kmean_ref):
    h = _rms_rows(x_ref[...], g_ref[...]).astype(BF16)
    pa = _dot(h, wa_ref[...])
    pb = _dot(h, wb_ref[...])
    bd = bd_ref[...]
    qk_normed = _head_rms(pa[:, :2 * MOBA_W], bd, gqk_ref[...])
    cq = _head_rms(pa[:, 3 * MOBA_W:3 * MOBA_W + MEM_W], bd, gmq_ref[...])
    pc = _dot(h, wc_ref[...])

    qk = _rope(qk_normed, cos_ref[...], sin_ref[...])
    moqt_ref[0] = (qk[:, :MOBA_W] * QK_SCALE).T.astype(BF16)
    mk = qk[:, MOBA_W:]
    mok_ref[...] = mk.astype(BF16)
    for blk in range(INPROJ_ROWS // MOBA_BLOCK):
        kmean_ref[0, blk:blk + 1, :] = jnp.mean(mk[blk * MOBA_BLOCK:(blk + 1) * MOBA_BLOCK], axis=0, keepdims=True)
    vt = pa[:, 2 * MOBA_W:3 * MOBA_W].T
    for hd in range(MOBA_HEADS):
        movt_ref[0, hd, 0:HEAD_DIM, :] = vt[hd * HEAD_DIM:(hd + 1) * HEAD_DIM, :].astype(BF16)
        movt_ref[0, hd, HEAD_DIM:, :] = jnp.ones((VT_ROWS - HEAD_DIM, vt.shape[1]), BF16)
    meq_ref[...] = (cq * QK_SCALE).astype(BF16)

    sb = jnp.concatenate([pa[:, 3 * MOBA_W + MEM_W:], pb, pc], axis=1)
    sbq_ref[...] = (sb[:, :SB_W] * QK_SCALE).astype(BF16)
    sbk_ref[...] = sb[:, SB_W:2 * SB_W].astype(BF16)
    sbv_ref[...] = sb[:, 2 * SB_W:].astype(BF16)


def _split_projection(w):
    sb = w[:, :3 * SB_W]
    rest = w[:, 3 * SB_W:]
    head = -rest.shape[1] % MXU_TILE
    mid = head + (sb.shape[1] - head) // (2 * MXU_TILE) * MXU_TILE
    return jnp.concatenate([rest, sb[:, :head]], axis=1), sb[:, head:mid], sb[:, mid:]


def _inproj(x2, g, ws, cos, sin, bd, gqk, gmq, seq):
    n, d = x2.shape
    tm = INPROJ_ROWS
    steps = n // tm
    per_seq = seq // tm
    blocks_per_step = tm // MOBA_BLOCK
    const = lambda i: (0, 0)
    rows = lambda i: (i, 0)
    out_shapes = [jax.ShapeDtypeStruct((n, SB_W), BF16)] * 3 + [
        jax.ShapeDtypeStruct((n // seq, MOBA_W, seq), BF16),
        jax.ShapeDtypeStruct((n, MOBA_W), BF16),
        jax.ShapeDtypeStruct((n // seq, MOBA_HEADS, VT_ROWS, seq), BF16),
        jax.ShapeDtypeStruct((n, MEM_W), BF16),
        jax.ShapeDtypeStruct((steps, blocks_per_step, MOBA_W), F32),
    ]
    out_specs = [pl.BlockSpec((tm, SB_W), rows)] * 3 + [
        pl.BlockSpec((1, MOBA_W, tm), lambda i: (i // per_seq, 0, i % per_seq)),
        pl.BlockSpec((tm, MOBA_W), rows),
        pl.BlockSpec((1, MOBA_HEADS, VT_ROWS, tm), lambda i: (i // per_seq, 0, 0, i % per_seq)),
        pl.BlockSpec((tm, MEM_W), rows),
        pl.BlockSpec((1, blocks_per_step, MOBA_W), lambda i: (i, 0, 0)),
    ]
    return pl.pallas_call(
        _inproj_kernel,
        grid=(steps,),
        in_specs=[
            pl.BlockSpec((tm, d), rows),
            pl.BlockSpec((1, d), const),
            *[pl.BlockSpec(w.shape, const) for w in ws],
            pl.BlockSpec((tm, LANES), lambda i: (i % per_seq, 0)),
            pl.BlockSpec((tm, LANES), lambda i: (i % per_seq, 0)),
            pl.BlockSpec((MXU_TILE, MXU_TILE), const),
            pl.BlockSpec((1, 2 * MOBA_W), const),
            pl.BlockSpec((1, MEM_W), const),
        ],
        out_specs=out_specs,
        out_shape=out_shapes,
        compiler_params=pltpu.CompilerParams(dimension_semantics=("arbitrary",), vmem_limit_bytes=VMEM_LIMIT),
        name="inproj",
    )(x2, g, *ws, cos, sin, bd, gqk, gmq)


def _sb_kernel(q_ref, k_ref, v_ref, o_ref):
    i = pl.program_id(1)
    lane = lax.broadcasted_iota(jnp.int32, (SB_SUB, LANES), 1)
    low = lane < HEAD_DIM
    kiota = lax.broadcasted_iota(jnp.int32, (SB_SUB, SB_WIN), 1)
    riota = lax.broadcasted_iota(jnp.int32, (SB_SUB, 1), 0)
    ur = lax.broadcasted_iota(jnp.int32, (SB_WIN, SB_WIN), 0)
    uc = lax.broadcasted_iota(jnp.int32, (SB_WIN, SB_WIN), 1)
    later = (ur > uc).astype(BF16)
    n_pairs = SB_W // LANES

    n_sub = SB_ROWS // SB_SUB
    chains = []
    for sub in range(n_sub):
        for pair in range(n_pairs):
            q2 = q_ref[0, sub * SB_SUB:(sub + 1) * SB_SUB, pair * LANES:(pair + 1) * LANES]
            zero = jnp.zeros_like(q2)
            chains += [(sub, pair, jnp.where(low, q2, zero)), (sub, pair, jnp.where(low, zero, q2))]
    n_chains = len(chains)

    def scan_windows(starts, limits, carries):
        def cols(c):
            return slice(chains[c][1] * LANES, (chains[c][1] + 1) * LANES)

        def scores(c):
            sub = chains[c][0]
            return _dot_nt(chains[c][2], k_ref[0, pl.ds(starts[sub], SB_WIN), cols(c)])

        valids = [(starts[sub] + kiota) < limits[sub] for sub in range(n_sub)]

        def scan(c, z):
            valid = valids[chains[c][0]]
            log_beta = jnp.minimum(z, 0.0) - jnp.log(1.0 + jnp.exp(-jnp.abs(z)))
            log_1m = jnp.where(valid, log_beta - z, 0.0)
            hi, lo = _split_bf16(log_1m)
            tail = _dot(hi, later) + _dot(lo, later)
            return log_beta, tail, carries[c] + jnp.sum(log_1m, axis=1, keepdims=True)

        def weigh(c, log_beta, tail):
            sub = chains[c][0]
            a = jnp.where(valids[sub], jnp.exp(log_beta + tail + carries[c]), 0.0)
            return _dot(a.astype(BF16), v_ref[0, pl.ds(starts[sub], SB_WIN), cols(c)])

        zs, scans, outs = {}, {}, [None] * n_chains
        for t in range(n_chains + 2 * SB_LOOKAHEAD):
            if t < n_chains:
                zs[t] = scores(t)
            c = t - SB_LOOKAHEAD
            if 0 <= c < n_chains:
                scans[c] = scan(c, zs.pop(c))
            c = t - 2 * SB_LOOKAHEAD
            if 0 <= c < n_chains:
                log_beta, tail, carry = scans.pop(c)
                outs[c] = (weigh(c, log_beta, tail), carry)
        return outs

    def unfinished(cs):
        worst = cs[0]
        for c in cs[1:]:
            worst = jnp.maximum(worst, c)
        return (jnp.max(worst) > SB_DONE_LOG).astype(jnp.int32)

    t0s = [i * SB_ROWS + sub * SB_SUB for sub in range(n_sub)]
    zero_carries = [jnp.zeros((SB_SUB, 1), F32)] * n_chains

    def window_before(ends):
        starts = [pl.multiple_of(jnp.maximum(e - SB_WIN, 0), SB_SUB) for e in ends]
        limits = [jnp.minimum(t0 + riota, e) for t0, e in zip(t0s, ends)]
        return starts, limits

    def write(accs):
        for sub in range(n_sub):
            for pair in range(n_pairs):
                c = (sub * n_pairs + pair) * 2
                out = jnp.where(low, accs[c], accs[c + 1])
                o_ref[0, sub * SB_SUB:(sub + 1) * SB_SUB, pair * LANES:(pair + 1) * LANES] = out.astype(BF16)

    ends0 = [t0 + SB_SUB for t0 in t0s]
    starts0, limits0 = window_before(ends0)
    first = scan_windows(starts0, limits0, zero_carries)
    write([acc for acc, _ in first])
    last_start = starts0[-1]

    @pl.when(jnp.logical_and(unfinished([carry for _, carry in first]) > 0, last_start > 0))
    def _():
        def cond(state):
            ends, more = state[:n_sub], state[n_sub]
            return jnp.logical_and(ends[-1] > 0, more > 0)

        def body(state):
            ends = state[:n_sub]
            accs_c = state[n_sub + 1:n_sub + 1 + n_chains]
            cs_c = state[n_sub + 1 + n_chains:]
            starts, limits = window_before(ends)
            res = scan_windows(starts, limits, list(cs_c))
            new_cs = [r[1] for r in res]
            return (*starts, unfinished(new_cs), *[a + r[0] for a, r in zip(accs_c, res)], *new_cs)

        zero_accs = [jnp.zeros((SB_SUB, LANES), F32)] * n_chains
        state = lax.while_loop(cond, body, (*ends0, jnp.int32(1), *zero_accs, *zero_carries))
        write(state[n_sub + 1:n_sub + 1 + n_chains])


def _sb_attention(q, k, v):
    b, s, w = q.shape
    return pl.pallas_call(
        _sb_kernel,
        grid=(b, s // SB_ROWS),
        in_specs=[
            pl.BlockSpec((1, SB_ROWS, w), lambda bi, i: (bi, i, 0)),
            pl.BlockSpec((1, s, w), lambda bi, i: (bi, 0, 0)),
            pl.BlockSpec((1, s, w), lambda bi, i: (bi, 0, 0)),
        ],
        out_specs=pl.BlockSpec((1, SB_ROWS, w), lambda bi, i: (bi, i, 0)),
        out_shape=jax.ShapeDtypeStruct((b, s, w), BF16),
        compiler_params=pltpu.CompilerParams(dimension_semantics=("arbitrary", "arbitrary"),
                                             vmem_limit_bytes=VMEM_LIMIT),
        name="sb_attention",
    )(q, k, v)


def _moba_kernel(qt_ref, k_ref, vt_ref, km_ref, gqk_ref, o_ref):
    first = pl.program_id(1) * MOBA_QBLOCKS
    nb = km_ref.shape[1]
    tq = MOBA_BLOCK
    n_heads = vt_ref.shape[1]
    blk = lax.broadcasted_iota(jnp.int32, (nb, tq), 0)
    items = [(qb, h) for qb in range(MOBA_QBLOCKS) for h in range(n_heads)]

    def cols(h):
        pair = h // 2
        return slice(pair * LANES, (pair + 1) * LANES)

    first_head_rows = lax.broadcasted_iota(jnp.int32, (LANES, MOBA_QBLOCKS * tq), 0) < HEAD_DIM
    qt_heads = []
    for pair in range(n_heads // 2):
        q2 = qt_ref[0, pair * LANES:(pair + 1) * LANES, :]
        zero = jnp.zeros_like(q2)
        qt_heads += [jnp.where(first_head_rows, q2, zero), jnp.where(first_head_rows, zero, q2)]
    qts = [qt_heads[h][:, qb * tq:(qb + 1) * tq] for qb, h in items]

    bits = [None] * len(items)

    def select_blocks():
        km_split = [_split_bf16(km_ref[0, :, cols(h)]) for h in range(n_heads)]
        gates = [_dot(km_split[h][0], qts[it]) + _dot(km_split[h][1], qts[it])
                 for it, (_, h) in enumerate(items)]
        blkf = blk.astype(F32)
        for it, (qb, _) in enumerate(items):
            gate = jnp.where(blk < first + qb, gates[it], -jnp.inf)
            chosen = jnp.zeros((1, tq), jnp.int32)
            for _ in range(MOBA_TOPK):
                best = jnp.max(gate, axis=0, keepdims=True)
                idx = jnp.min(jnp.where(gate == best, blkf, float(nb)), axis=0, keepdims=True)
                chosen = chosen | jnp.where(best > -jnp.inf, jnp.left_shift(1, idx.astype(jnp.int32)), 0)
                gate = jnp.where(blkf == idx, -jnp.inf, gate)
            bits[it] = chosen

    def block_offset(j):
        return pl.multiple_of(j * tq, tq)

    def pipelined(work, scores, reduce, lookahead=MOBA_LOOKAHEAD):
        queued = {n: scores(w) for n, w in enumerate(work[:lookahead])}
        for n, w in enumerate(work):
            if n + lookahead < len(work):
                queued[n + lookahead] = scores(work[n + lookahead])
            reduce(w, queued.pop(n))

    causal = lax.broadcasted_iota(jnp.int32, (tq, tq), 0) <= lax.broadcasted_iota(jnp.int32, (tq, tq), 1)
    n_items = len(items)
    all_items = list(range(n_items))

    def own_scores(it):
        qb, h = items[it]
        return _dot(k_ref[0, pl.ds(block_offset(first + qb), tq), cols(h)], qts[it])

    def past_scores(blocks_of):
        def scores(it):
            h = items[it][1]
            return [_dot(k_ref[0, pl.ds(block_offset(j), tq), cols(h)], qts[it]) for j in blocks_of(it)]
        return scores

    def v_rows(it, j):
        return vt_ref[0, items[it][1], :, pl.ds(block_offset(j), tq)]

    def picked(it, j):
        return (jnp.right_shift(bits[it], j) & 1) > 0

    def earlier_in_step(it):
        return [first + e for e in range(items[it][0])]

    def loop_blocks(jj):
        return lambda it: [MOBA_LOOP_BLOCKS * jj + u for u in range(MOBA_LOOP_BLOCKS)]

    loop_trips = first // MOBA_LOOP_BLOCKS

    def with_earlier(qb):
        return [it for it in all_items if items[it][0] == qb]

    def stable_path():
        state = [None] * (2 * n_items)
        select_blocks()

        def own_reduce(it, raw):
            s = jnp.where(causal, raw, NEG_BIG)
            m = jnp.max(s, axis=0, keepdims=True)
            state[2 * it] = m
            state[2 * it + 1] = _dot(v_rows(it, first + items[it][0]), jnp.exp(s - m).astype(BF16))

        pipelined(all_items, own_scores, own_reduce)

        def attend(st, work, blocks_of):
            def reduce(it, raws):
                m, acc = st[2 * it], st[2 * it + 1]
                picks = [picked(it, j) for j in blocks_of(it)]
                m_new = m
                for pick, s in zip(picks, raws):
                    m_new = jnp.maximum(m_new, jnp.where(pick, jnp.max(s, axis=0, keepdims=True), NEG_BIG))
                acc = jnp.exp(m - m_new) * acc
                for j, pick, s in zip(blocks_of(it), picks, raws):
                    p = jnp.exp((s - m_new).astype(BF16))
                    acc = acc + _dot(v_rows(it, j), jnp.where(pick, p, jnp.zeros_like(p)))
                st[2 * it] = m_new
                st[2 * it + 1] = acc
            pipelined(work, past_scores(blocks_of), reduce)

        for qb in range(1, MOBA_QBLOCKS):
            attend(state, with_earlier(qb), earlier_in_step)

        def body(jj, st):
            st = list(st)
            attend(st, all_items, loop_blocks(jj))
            return tuple(st)

        state = lax.fori_loop(0, loop_trips, body, tuple(state))
        return tuple(state[1::2])

    def bounded_path():
        def attend(st, blocks_of):
            work = []
            for it in all_items:
                blocks = blocks_of(it)
                work += [(it, blocks[c:c + MOBA_ENTRY_BLOCKS]) for c in range(0, len(blocks), MOBA_ENTRY_BLOCKS)]

            def scores(entry):
                it, blocks = entry
                return [_dot(k_ref[0, pl.ds(block_offset(j), tq), cols(items[it][1])], qts[it]) for j, _ in blocks]

            def reduce(entry, raws):
                it, blocks = entry
                for (j, is_own), s in zip(blocks, raws):
                    p = jnp.exp(s).astype(BF16)
                    p = jnp.where(causal if is_own else picked(it, j), p, jnp.zeros_like(p))
                    term = _dot(v_rows(it, j), p)
                    st[it] = term if st[it] is None else st[it] + term

            pipelined(work, scores, reduce, MOBA_ENTRY_LOOKAHEAD)

        select_blocks()
        accs = [None] * n_items
        attend(accs, lambda it: [(j, False) for j in earlier_in_step(it)] + [(first + items[it][0], True)])

        def body(jj, st):
            st = list(st)
            attend(st, lambda it: [(j, False) for j in loop_blocks(jj)(it)])
            return tuple(st)

        return lax.fori_loop(0, loop_trips, body, tuple(accs))

    gq = gqk_ref[:, :MOBA_W]
    gk = gqk_ref[:, MOBA_W:]
    score_bound_sq = (HEAD_DIM * SCORE_BOUND_MARGIN) * jnp.max(gq * gq) * jnp.max(gk * gk)
    accs = lax.cond(score_bound_sq <= MOBA_SAFE_LOGIT * MOBA_SAFE_LOGIT, bounded_path, stable_path)
    for qb in range(MOBA_QBLOCKS):
        for pair in range(n_heads // 2):
            outs = []
            for h in (2 * pair, 2 * pair + 1):
                acc = accs[qb * n_heads + h]
                outs.append(acc[:HEAD_DIM] / acc[HEAD_DIM:HEAD_DIM + 1])
            o_ref[0, qb * tq:(qb + 1) * tq, pair * LANES:(pair + 1) * LANES] = (
                jnp.concatenate(outs, axis=0).T.astype(BF16))


def _moba_attention(qt, k, vt, kmean, gqk):
    b, s, w = k.shape
    nb = s // MOBA_BLOCK
    rows = MOBA_QBLOCKS * MOBA_BLOCK
    return pl.pallas_call(
        _moba_kernel,
        grid=(b, nb // MOBA_QBLOCKS),
        in_specs=[
            pl.BlockSpec((1, w, rows), lambda bi, i: (bi, 0, i)),
            pl.BlockSpec((1, s, w), lambda bi, i: (bi, 0, 0)),
            pl.BlockSpec((1, w // HEAD_DIM, VT_ROWS, s), lambda bi, i: (bi, 0, 0, 0)),
            pl.BlockSpec((1, nb, w), lambda bi, i: (bi, 0, 0)),
            pl.BlockSpec((1, 2 * w), lambda bi, i: (0, 0)),
        ],
        out_specs=pl.BlockSpec((1, rows, w), lambda bi, i: (bi, i, 0)),
        out_shape=jax.ShapeDtypeStruct((b, s, w), BF16),
        compiler_params=pltpu.CompilerParams(dimension_semantics=("arbitrary",) * 2, vmem_limit_bytes=VMEM_LIMIT),
        name="moba_attention",
    )(qt, k, vt, kmean, gqk)


def _memattn_kernel(q_ref, ck_ref, cvt_ref, o_ref):
    tq = MXU_TILE
    lane = lax.broadcasted_iota(jnp.int32, (tq, LANES), 1)
    low = lane < HEAD_DIM
    work = [(tile, hd) for tile in range(q_ref.shape[1] // tq) for hd in range(MEM_HEADS)]

    def cols(hd):
        return slice((hd // 2) * LANES, (hd // 2 + 1) * LANES)

    def scores(w):
        tile, hd = work[w]
        q2 = q_ref[0, tile * tq:(tile + 1) * tq, cols(hd)]
        zero = jnp.zeros_like(q2)
        qm = jnp.where(low, q2, zero) if hd % 2 == 0 else jnp.where(low, zero, q2)
        return _dot_nt(ck_ref[0, :, cols(hd)], qm)

    queued = {w: scores(w) for w in range(MEMATTN_LOOKAHEAD)}
    outs = {}
    for w, (tile, hd) in enumerate(work):
        if w + MEMATTN_LOOKAHEAD < len(work):
            queued[w + MEMATTN_LOOKAHEAD] = scores(w + MEMATTN_LOOKAHEAD)
        s = queued.pop(w)
        p = jnp.exp(s - jnp.max(s, axis=0, keepdims=True)).astype(BF16)
        acc = _dot(cvt_ref[0, hd], p)
        outs[hd % 2] = acc[:HEAD_DIM] / acc[HEAD_DIM:HEAD_DIM + 1]
        if hd % 2 == 1:
            o_ref[0, tile * tq:(tile + 1) * tq, cols(hd)] = (
                jnp.concatenate([outs[0], outs[1]], axis=0).T.astype(BF16))


def _mem_attention(q, ck, cv):
    b, s, w = q.shape
    m = ck.shape[1]
    tm = MEMATTN_ROWS
    return pl.pallas_call(
        _memattn_kernel,
        grid=(b, s // tm),
        in_specs=[
            pl.BlockSpec((1, tm, w), lambda bi, i: (bi, i, 0)),
            pl.BlockSpec((1, m, w), lambda bi, i: (bi, 0, 0)),
            pl.BlockSpec((1, MEM_HEADS, VT_ROWS, m), lambda bi, i: (bi, 0, 0, 0)),
        ],
        out_specs=pl.BlockSpec((1, tm, w), lambda bi, i: (bi, i, 0)),
        out_shape=jax.ShapeDtypeStruct((b, s, w), BF16),
        compiler_params=pltpu.CompilerParams(dimension_semantics=("arbitrary", "arbitrary"),
                                             vmem_limit_bytes=VMEM_LIMIT),
        name="mem_attention",
    )(q, ck, cv)


def _sigmoid(t):
    return 1.0 / (1.0 + jnp.exp(-t))


def _tail_kernel(x_ref, osb_ref, omo_ref, ome_ref, gmix_ref, gffn_ref, wg_ref, wsb_ref, wmo_ref, wme_ref,
                 wout_ref, wfi_ref, wfd_ref, out_ref):
    x = x_ref[...]
    d = x.shape[1]
    h = _rms_rows(x, gmix_ref[...]).astype(BF16)
    mix = None
    for n, (o_ref, wu_ref) in enumerate(((osb_ref, wsb_ref), (omo_ref, wmo_ref), (ome_ref, wme_ref))):
        gate = _dot(h, wg_ref[:, n * d:(n + 1) * d])
        term = _sigmoid(gate) * _dot(o_ref[...], wu_ref[...])
        mix = term if mix is None else mix + term
    x1 = x + _dot(mix.astype(BF16), wout_ref[...])
    h2 = _rms_rows(x1, gffn_ref[...]).astype(BF16)
    dff = wfd_ref.shape[0]
    gate = _dot(h2, wfi_ref[:, :dff])
    up = _dot(h2, wfi_ref[:, dff:])
    ff = (gate * _sigmoid(gate) * up).astype(BF16)
    out_ref[...] = x1 + _dot(ff, wfd_ref[...])


def _tail(x2, osb, omo, ome, gmix, gffn, wg, wsb, wmo, wme, wout, wfi, wfd):
    n, d = x2.shape
    tm = TAIL_ROWS
    rows = lambda i: (i, 0)

    def resident(arr):
        return pl.BlockSpec(arr.shape, lambda i: (0, 0), pipeline_mode=pl.Buffered(1))

    return pl.pallas_call(
        _tail_kernel,
        grid=(n // tm,),
        in_specs=[
            pl.BlockSpec((tm, d), rows),
            pl.BlockSpec((tm, SB_W), rows),
            pl.BlockSpec((tm, MOBA_W), rows),
            pl.BlockSpec((tm, MEM_W), rows),
            resident(gmix), resident(gffn), resident(wg), resident(wsb), resident(wmo), resident(wme),
            resident(wout), resident(wfi), resident(wfd),
        ],
        out_specs=pl.BlockSpec((tm, d), rows),
        out_shape=jax.ShapeDtypeStruct((n, d), F32),
        compiler_params=pltpu.CompilerParams(dimension_semantics=("arbitrary",), vmem_limit_bytes=VMEM_LIMIT),
        name="merge_out_ffn",
    )(x2, osb, omo, ome, gmix, gffn, wg, wsb, wmo, wme, wout, wfi, wfd)


def _rope_tables(seq):
    half = HEAD_DIM // 2
    inv_freq = ROPE_THETA ** (-jnp.arange(half, dtype=F32) * 2.0 / HEAD_DIM)
    ang = jnp.arange(seq, dtype=F32)[:, None] * inv_freq[None, :]
    cos, sin = jnp.cos(ang), jnp.sin(ang)
    reps = LANES // HEAD_DIM
    return jnp.tile(jnp.concatenate([cos, cos], axis=1), (1, reps)), jnp.tile(jnp.concatenate([-sin, sin], axis=1), (1, reps))


def _head_mean_matrix(width):
    r = jnp.arange(width) // HEAD_DIM
    return jnp.where(r[:, None] == r[None, :], 1.0 / HEAD_DIM, 0.0).astype(BF16)


def _layer(x, mem, mix_norm_g, mem_norm_g, ffn_norm_g, w_in, w_mem_kv, moba_q_norm_g, moba_k_norm_g,
           mem_q_norm_g, mem_k_norm_g, w_up_sb, w_up_moba, w_up_mem, w_out, w_ffn_in, w_ffn_down):
    b, s, d = x.shape
    x2 = x.reshape(b * s, d)
    cos, sin = _rope_tables(s)
    bd = _head_mean_matrix(MXU_TILE)
    row = lambda g, reps=1: jnp.tile(g.astype(F32), reps)[None, :]
    gqk = jnp.concatenate([row(moba_q_norm_g, MOBA_HEADS), row(moba_k_norm_g, MOBA_HEADS)], axis=1)

    ck, cv = _memkv(mem, row(mem_norm_g), w_mem_kv.astype(BF16), bd, row(mem_k_norm_g, MEM_HEADS))
    sbq, sbk, sbv, moqt, mok, movt, meq, kmean = _inproj(
        x2, row(mix_norm_g), _split_projection(w_in[:, :QKV_W].astype(BF16)), cos, sin, bd, gqk,
        row(mem_q_norm_g, MEM_HEADS), s)
    seq3 = lambda t: t.reshape(b, s, t.shape[-1])
    o_sb = _sb_attention(seq3(sbq), seq3(sbk), seq3(sbv))
    o_mo = _moba_attention(moqt, seq3(mok), movt, kmean.reshape(b, s // MOBA_BLOCK, MOBA_W), gqk)
    o_me = _mem_attention(seq3(meq), ck, cv)
    flat = lambda t: t.reshape(b * s, t.shape[-1])
    out = _tail(x2, flat(o_sb), flat(o_mo), flat(o_me), row(mix_norm_g), row(ffn_norm_g),
                w_in[:, QKV_W:].astype(BF16), w_up_sb.astype(BF16), w_up_moba.astype(BF16), w_up_mem.astype(BF16),
                w_out.astype(BF16), w_ffn_in.astype(BF16), w_ffn_down.astype(BF16))
    return out.reshape(b, s, d)


def kernel(x, mem, mix_norm_g, mem_norm_g, ffn_norm_g, w_in, w_mem_kv, moba_q_norm_g, moba_k_norm_g, mem_q_norm_g, mem_k_norm_g, w_up_sb, w_up_moba, w_up_mem, w_out, w_ffn_in, w_ffn_down):
    for l in range(w_in.shape[0]):
        x = _layer(x, mem, mix_norm_g[l], mem_norm_g[l], ffn_norm_g[l], w_in[l], w_mem_kv[l],
                   moba_q_norm_g[l], moba_k_norm_g[l], mem_q_norm_g[l], mem_k_norm_g[l],
                   w_up_sb[l], w_up_moba[l], w_up_mem[l], w_out[l], w_ffn_in[l], w_ffn_down[l])
    return x
```

```python
import functools
import math

import jax
import jax.numpy as jnp
from jax import lax
from jax.experimental import pallas as pl
from jax.experimental.pallas import tpu as pltpu

F32 = jnp.float32
BF16 = jnp.bfloat16

HEAD_DIM = 64
SB_HEADS = 6
MOBA_HEADS = 6
MEM_HEADS = 4
SB_W = SB_HEADS * HEAD_DIM
MOBA_W = MOBA_HEADS * HEAD_DIM
MEM_W = MEM_HEADS * HEAD_DIM
QKV_W = 3 * SB_W + 3 * MOBA_W + MEM_W
MOBA_BLOCK = 256
MOBA_TOPK = 3
ROPE_THETA = 10000.0
EPS = 1e-6
QK_SCALE = 1.0 / math.sqrt(HEAD_DIM)

LANES = 128
MXU_TILE = 256
BF16_SUBLANES = 16
VT_ROWS = HEAD_DIM + BF16_SUBLANES
NEG_BIG = -1e30

SB_DONE_LOG = -30.0

INPROJ_ROWS = 512
SB_ROWS = 512
SB_SUB = 64
SB_WIN = 128
SB_LOOKAHEAD = 10
MOBA_LOOKAHEAD = 2
MOBA_QBLOCKS = 4
MOBA_LOOP_BLOCKS = 4
MOBA_ENTRY_BLOCKS = 2
MOBA_ENTRY_LOOKAHEAD = 3
MOBA_SAFE_LOGIT = 40.0
SCORE_BOUND_MARGIN = 1.1
MEMATTN_ROWS = 1024
MEMATTN_LOOKAHEAD = 5
TAIL_ROWS = 512
VMEM_LIMIT = 56 * 1024 * 1024


def _dot(a, b):
    return jnp.dot(a, b, preferred_element_type=F32)


def _dot_nt(a, b):
    return lax.dot_general(a, b, (((1,), (1,)), ((), ())), preferred_element_type=F32)


def _split_bf16(t):
    hi = t.astype(BF16)
    lo = (t - hi.astype(F32)).astype(BF16)
    return hi, lo


def _rms_rows(x, g):
    ms = jnp.mean(x * x, axis=-1, keepdims=True)
    return x * lax.rsqrt(ms + EPS) * g


def _head_rms(t, bd, g):
    hi, lo = _split_bf16(t * t)
    parts = []
    for c in range(0, t.shape[1], MXU_TILE):
        w = min(MXU_TILE, t.shape[1] - c)
        parts.append(_dot(hi[:, c:c + w], bd[:w, :w]) + _dot(lo[:, c:c + w], bd[:w, :w]))
    ms = parts[0] if len(parts) == 1 else jnp.concatenate(parts, axis=1)
    return t * lax.rsqrt(ms + EPS) * g


def _rotate_half_pairs(t):
    lane = lax.broadcasted_iota(jnp.int32, t.shape, 1)
    first_half = (lane & (HEAD_DIM - 1)) < (HEAD_DIM // 2)
    return jnp.where(first_half, pltpu.roll(t, LANES - HEAD_DIM // 2, 1), pltpu.roll(t, HEAD_DIM // 2, 1))


def _rope(t, cos, sin_signed):
    outs = []
    for c in range(t.shape[1] // LANES):
        tc = t[:, c * LANES:(c + 1) * LANES]
        outs.append(tc * cos + _rotate_half_pairs(tc) * sin_signed)
    return jnp.concatenate(outs, axis=1)


def _memkv_kernel(mem_ref, g_ref, w_ref, bd_ref, gk_ref, ck_ref, cvt_ref):
    h = _rms_rows(mem_ref[0], g_ref[...]).astype(BF16)
    kv = _dot(h, w_ref[...])
    k = kv[:, :MEM_W]
    ck_ref[0] = _head_rms(k, bd_ref[...], gk_ref[...]).astype(BF16)
    vt = kv[:, MEM_W:].T
    for hd in range(MEM_HEADS):
        cvt_ref[0, hd, 0:HEAD_DIM, :] = vt[hd * HEAD_DIM:(hd + 1) * HEAD_DIM, :].astype(BF16)
        cvt_ref[0, hd, HEAD_DIM:, :] = jnp.ones((VT_ROWS - HEAD_DIM, vt.shape[1]), BF16)


def _memkv(mem, g, w, bd, gk):
    b, m, d = mem.shape
    return pl.pallas_call(
        _memkv_kernel,
        grid=(b,),
        in_specs=[
            pl.BlockSpec((1, m, d), lambda i: (i, 0, 0)),
            pl.BlockSpec((1, d), lambda i: (0, 0)),
            pl.BlockSpec((d, 2 * MEM_W), lambda i: (0, 0)),
            pl.BlockSpec((MXU_TILE, MXU_TILE), lambda i: (0, 0)),
            pl.BlockSpec((1, MEM_W), lambda i: (0, 0)),
        ],
        out_specs=[
            pl.BlockSpec((1, m, MEM_W), lambda i: (i, 0, 0)),
            pl.BlockSpec((1, MEM_HEADS, VT_ROWS, m), lambda i: (i, 0, 0, 0)),
        ],
        out_shape=[jax.ShapeDtypeStruct((b, m, MEM_W), BF16),
                   jax.ShapeDtypeStruct((b, MEM_HEADS, VT_ROWS, m), BF16)],
        compiler_params=pltpu.CompilerParams(dimension_semantics=("arbitrary",), vmem_limit_bytes=VMEM_LIMIT),
        name="memkv",
    )(mem, g, w, bd, gk)


def _inproj_kernel(x_ref, g_ref, wa_ref, wb_ref, wc_ref, cos_ref, sin_ref, bd_ref, gqk_ref, gmq_ref,
                   sbq_ref, sbk_ref, sbv_ref, moqt_ref, mok_ref, movt_ref, meq_ref, kmean_ref):
    h = _rms_rows(x_ref[...], g_ref[...]).astype(BF16)
    pa = _dot(h, wa_ref[...])
    pb = _dot(h, wb_ref[...])
    bd = bd_ref[...]
    qk_normed = _head_rms(pa[:, :2 * MOBA_W], bd, gqk_ref[...])
    cq = _head_rms(pa[:, 3 * MOBA_W:3 * MOBA_W + MEM_W], bd, gmq_ref[...])
    pc = _dot(h, wc_ref[...])

    qk = _rope(qk_normed, cos_ref[...], sin_ref[...])
    moqt_ref[0] = (qk[:, :MOBA_W] * QK_SCALE).T.astype(BF16)
    mk = qk[:, MOBA_W:]
    mok_ref[...] = mk.astype(BF16)
    for blk in range(INPROJ_ROWS // MOBA_BLOCK):
        kmean_ref[0, blk:blk + 1, :] = jnp.mean(mk[blk * MOBA_BLOCK:(blk + 1) * MOBA_BLOCK], axis=0, keepdims=True)
    vt = pa[:, 2 * MOBA_W:3 * MOBA_W].T
    for hd in range(MOBA_HEADS):
        movt_ref[0, hd, 0:HEAD_DIM, :] = vt[hd * HEAD_DIM:(hd + 1) * HEAD_DIM, :].astype(BF16)
        movt_ref[0, hd, HEAD_DIM:, :] = jnp.ones((VT_ROWS - HEAD_DIM, vt.shape[1]), BF16)
    meq_ref[...] = (cq * QK_SCALE).astype(BF16)

    sb = jnp.concatenate([pa[:, 3 * MOBA_W + MEM_W:], pb, pc], axis=1)
    sbq_ref[...] = (sb[:, :SB_W] * QK_SCALE).astype(BF16)
    sbk_ref[...] = sb[:, SB_W:2 * SB_W].astype(BF16)
    sbv_ref[...] = sb[:, 2 * SB_W:].astype(BF16)


def _split_projection(w):
    sb = w[:, :3 * SB_W]
    rest = w[:, 3 * SB_W:]
    head = -rest.shape[1] % MXU_TILE
    mid = head + (sb.shape[1] - head) // (2 * MXU_TILE) * MXU_TILE
    return jnp.concatenate([rest, sb[:, :head]], axis=1), sb[:, head:mid], sb[:, mid:]


def _inproj(x2, g, ws, cos, sin, bd, gqk, gmq, seq):
    n, d = x2.shape
    tm = INPROJ_ROWS
    steps = n // tm
    per_seq = seq // tm
    blocks_per_step = tm // MOBA_BLOCK
    const = lambda i: (0, 0)
    rows = lambda i: (i, 0)
    out_shapes = [jax.ShapeDtypeStruct((n, SB_W), BF16)] * 3 + [
        jax.ShapeDtypeStruct((n // seq, MOBA_W, seq), BF16),
        jax.ShapeDtypeStruct((n, MOBA_W), BF16),
        jax.ShapeDtypeStruct((n // seq, MOBA_HEADS, VT_ROWS, seq), BF16),
        jax.ShapeDtypeStruct((n, MEM_W), BF16),
        jax.ShapeDtypeStruct((steps, blocks_per_step, MOBA_W), F32),
    ]
    out_specs = [pl.BlockSpec((tm, SB_W), rows)] * 3 + [
        pl.BlockSpec((1, MOBA_W, tm), lambda i: (i // per_seq, 0, i % per_seq)),
        pl.BlockSpec((tm, MOBA_W), rows),
        pl.BlockSpec((1, MOBA_HEADS, VT_ROWS, tm), lambda i: (i // per_seq, 0, 0, i % per_seq)),
        pl.BlockSpec((tm, MEM_W), rows),
        pl.BlockSpec((1, blocks_per_step, MOBA_W), lambda i: (i, 0, 0)),
    ]
    return pl.pallas_call(
        _inproj_kernel,
        grid=(steps,),
        in_specs=[
            pl.BlockSpec((tm, d), rows),
            pl.BlockSpec((1, d), const),
            *[pl.BlockSpec(w.shape, const) for w in ws],
            pl.BlockSpec((tm, LANES), lambda i: (i % per_seq, 0)),
            pl.BlockSpec((tm, LANES), lambda i: (i % per_seq, 0)),
            pl.BlockSpec((MXU_TILE, MXU_TILE), const),
            pl.BlockSpec((1, 2 * MOBA_W), const),
            pl.BlockSpec((1, MEM_W), const),
        ],
        out_specs=out_specs,
        out_shape=out_shapes,
        compiler_params=pltpu.CompilerParams(dimension_semantics=("arbitrary",), vmem_limit_bytes=VMEM_LIMIT),
        name="inproj",
    )(x2, g, *ws, cos, sin, bd, gqk, gmq)


def _sb_kernel(q_ref, k_ref, v_ref, o_ref):
    i = pl.program_id(1)
    lane = lax.broadcasted_iota(jnp.int32, (SB_SUB, LANES), 1)
    low = lane < HEAD_DIM
    kiota = lax.broadcasted_iota(jnp.int32, (SB_SUB, SB_WIN), 1)
    riota = lax.broadcasted_iota(jnp.int32, (SB_SUB, 1), 0)
    ur = lax.broadcasted_iota(jnp.int32, (SB_WIN, SB_WIN), 0)
    uc = lax.broadcasted_iota(jnp.int32, (SB_WIN, SB_WIN), 1)
    later = (ur > uc).astype(BF16)
    n_pairs = SB_W // LANES

    n_sub = SB_ROWS // SB_SUB
    chains = []
    for sub in range(n_sub):
        for pair in range(n_pairs):
            q2 = q_ref[0, sub * SB_SUB:(sub + 1) * SB_SUB, pair * LANES:(pair + 1) * LANES]
            zero = jnp.zeros_like(q2)
            chains += [(sub, pair, jnp.where(low, q2, zero)), (sub, pair, jnp.where(low, zero, q2))]
    n_chains = len(chains)

    def scan_windows(starts, limits, carries):
        def cols(c):
            return slice(chains[c][1] * LANES, (chains[c][1] + 1) * LANES)

        def scores(c):
            sub = chains[c][0]
            return _dot_nt(chains[c][2], k_ref[0, pl.ds(starts[sub], SB_WIN), cols(c)])

        valids = [(starts[sub] + kiota) < limits[sub] for sub in range(n_sub)]

        def scan(c, z):
            valid = valids[chains[c][0]]
            log_beta = jnp.minimum(z, 0.0) - jnp.log(1.0 + jnp.exp(-jnp.abs(z)))
            log_1m = jnp.where(valid, log_beta - z, 0.0)
            hi, lo = _split_bf16(log_1m)
            tail = _dot(hi, later) + _dot(lo, later)
            return log_beta, tail, carries[c] + jnp.sum(log_1m, axis=1, keepdims=True)

        def weigh(c, log_beta, tail):
            sub = chains[c][0]
            a = jnp.where(valids[sub], jnp.exp(log_beta + tail + carries[c]), 0.0)
            return _dot(a.astype(BF16), v_ref[0, pl.ds(starts[sub], SB_WIN), cols(c)])

        zs, scans, outs = {}, {}, [None] * n_chains
        for t in range(n_chains + 2 * SB_LOOKAHEAD):
            if t < n_chains:
                zs[t] = scores(t)
            c = t - SB_LOOKAHEAD
            if 0 <= c < n_chains:
                scans[c] = scan(c, zs.pop(c))
            c = t - 2 * SB_LOOKAHEAD
            if 0 <= c < n_chains:
                log_beta, tail, carry = scans.pop(c)
                outs[c] = (weigh(c, log_beta, tail), carry)
        return outs

    def unfinished(cs, starts):
        worst = None
        for c, carry in enumerate(cs):
            open_carry = jnp.where(starts[chains[c][0]] > 0, carry, NEG_BIG)
            worst = open_carry if worst is None else jnp.maximum(worst, open_carry)
        return (jnp.max(worst) > SB_DONE_LOG).astype(jnp.int32)

    t0s = [i * SB_ROWS + sub * SB_SUB for sub in range(n_sub)]
    zero_carries = [jnp.zeros((SB_SUB, 1), F32)] * n_chains

    def window_before(ends):
        starts = [pl.multiple_of(jnp.maximum(e - SB_WIN, 0), SB_SUB) for e in ends]
        limits = [jnp.minimum(t0 + riota, e) for t0, e in zip(t0s, ends)]
        return starts, limits

    def write(accs):
        for sub in range(n_sub):
            for pair in range(n_pairs):
                c = (sub * n_pairs + pair) * 2
                out = jnp.where(low, accs[c], accs[c + 1])
                o_ref[0, sub * SB_SUB:(sub + 1) * SB_SUB, pair * LANES:(pair + 1) * LANES] = out.astype(BF16)

    ends0 = [t0 + SB_SUB for t0 in t0s]
    starts0, limits0 = window_before(ends0)
    first = scan_windows(starts0, limits0, zero_carries)
    write([acc for acc, _ in first])
    @pl.when(unfinished([carry for _, carry in first], starts0) > 0)
    def _():
        def cond(state):
            ends, more = state[:n_sub], state[n_sub]
            return jnp.logical_and(ends[-1] > 0, more > 0)

        def body(state):
            ends = state[:n_sub]
            accs_c = state[n_sub + 1:n_sub + 1 + n_chains]
            cs_c = state[n_sub + 1 + n_chains:]
            starts, limits = window_before(ends)
            res = scan_windows(starts, limits, list(cs_c))
            new_cs = [r[1] for r in res]
            return (*starts, unfinished(new_cs, starts), *[a + r[0] for a, r in zip(accs_c, res)], *new_cs)

        zero_accs = [jnp.zeros((SB_SUB, LANES), F32)] * n_chains
        state = lax.while_loop(cond, body, (*ends0, jnp.int32(1), *zero_accs, *zero_carries))
        write(state[n_sub + 1:n_sub + 1 + n_chains])


def _sb_attention(q, k, v):
    b, s, w = q.shape
    return pl.pallas_call(
        _sb_kernel,
        grid=(b, s // SB_ROWS),
        in_specs=[
            pl.BlockSpec((1, SB_ROWS, w), lambda bi, i: (bi, i, 0)),
            pl.BlockSpec((1, s, w), lambda bi, i: (bi, 0, 0)),
            pl.BlockSpec((1, s, w), lambda bi, i: (bi, 0, 0)),
        ],
        out_specs=pl.BlockSpec((1, SB_ROWS, w), lambda bi, i: (bi, i, 0)),
        out_shape=jax.ShapeDtypeStruct((b, s, w), BF16),
        compiler_params=pltpu.CompilerParams(dimension_semantics=("arbitrary", "arbitrary"),
                                             vmem_limit_bytes=VMEM_LIMIT),
        name="sb_attention",
    )(q, k, v)


def _moba_kernel(qt_ref, k_ref, vt_ref, km_ref, gqk_ref, o_ref):
    first = pl.program_id(1) * MOBA_QBLOCKS
    nb = km_ref.shape[1]
    tq = MOBA_BLOCK
    n_heads = vt_ref.shape[1]
    blk = lax.broadcasted_iota(jnp.int32, (nb, tq), 0)
    items = [(qb, h) for qb in range(MOBA_QBLOCKS) for h in range(n_heads)]

    def cols(h):
        pair = h // 2
        return slice(pair * LANES, (pair + 1) * LANES)

    first_head_rows = lax.broadcasted_iota(jnp.int32, (LANES, MOBA_QBLOCKS * tq), 0) < HEAD_DIM
    qt_heads = []
    for pair in range(n_heads // 2):
        q2 = qt_ref[0, pair * LANES:(pair + 1) * LANES, :]
        zero = jnp.zeros_like(q2)
        qt_heads += [jnp.where(first_head_rows, q2, zero), jnp.where(first_head_rows, zero, q2)]
    qts = [qt_heads[h][:, qb * tq:(qb + 1) * tq] for qb, h in items]

    bits = [None] * len(items)

    def select_blocks():
        km_split = [_split_bf16(km_ref[0, :, cols(h)]) for h in range(n_heads)]
        gates = [_dot(km_split[h][0], qts[it]) + _dot(km_split[h][1], qts[it])
                 for it, (_, h) in enumerate(items)]
        blkf = blk.astype(F32)
        for it, (qb, _) in enumerate(items):
            gate = jnp.where(blk < first + qb, gates[it], -jnp.inf)
            chosen = jnp.zeros((1, tq), jnp.int32)
            for _ in range(MOBA_TOPK):
                best = jnp.max(gate, axis=0, keepdims=True)
                idx = jnp.min(jnp.where(gate == best, blkf, float(nb)), axis=0, keepdims=True)
                chosen = chosen | jnp.where(best > -jnp.inf, jnp.left_shift(1, idx.astype(jnp.int32)), 0)
                gate = jnp.where(blkf == idx, -jnp.inf, gate)
            bits[it] = chosen

    def block_offset(j):
        return pl.multiple_of(j * tq, tq)

    def pipelined(work, scores, reduce, lookahead=MOBA_LOOKAHEAD):
        queued = {n: scores(w) for n, w in enumerate(work[:lookahead])}
        for n, w in enumerate(work):
            if n + lookahead < len(work):
                queued[n + lookahead] = scores(work[n + lookahead])
            reduce(w, queued.pop(n))

    causal = lax.broadcasted_iota(jnp.int32, (tq, tq), 0) <= lax.broadcasted_iota(jnp.int32, (tq, tq), 1)
    n_items = len(items)
    all_items = list(range(n_items))

    def own_scores(it):
        qb, h = items[it]
        return _dot(k_ref[0, pl.ds(block_offset(first + qb), tq), cols(h)], qts[it])

    def past_scores(blocks_of):
        def scores(it):
            h = items[it][1]
            return [_dot(k_ref[0, pl.ds(block_offset(j), tq), cols(h)], qts[it]) for j in blocks_of(it)]
        return scores

    def v_rows(it, j):
        return vt_ref[0, items[it][1], :, pl.ds(block_offset(j), tq)]

    def picked(it, j):
        return (jnp.right_shift(bits[it], j) & 1) > 0

    def earlier_in_step(it):
        return [first + e for e in range(items[it][0])]

    def loop_blocks(jj):
        return lambda it: [MOBA_LOOP_BLOCKS * jj + u for u in range(MOBA_LOOP_BLOCKS)]

    loop_trips = first // MOBA_LOOP_BLOCKS

    def with_earlier(qb):
        return [it for it in all_items if items[it][0] == qb]

    def stable_path():
        state = [None] * (2 * n_items)
        select_blocks()

        def own_reduce(it, raw):
            s = jnp.where(causal, raw, NEG_BIG)
            m = jnp.max(s, axis=0, keepdims=True)
            state[2 * it] = m
            state[2 * it + 1] = _dot(v_rows(it, first + items[it][0]), jnp.exp(s - m).astype(BF16))

        pipelined(all_items, own_scores, own_reduce)

        def attend(st, work, blocks_of):
            def reduce(it, raws):
                m, acc = st[2 * it], st[2 * it + 1]
                picks = [picked(it, j) for j in blocks_of(it)]
                m_new = m
                for pick, s in zip(picks, raws):
                    m_new = jnp.maximum(m_new, jnp.where(pick, jnp.max(s, axis=0, keepdims=True), NEG_BIG))
                acc = jnp.exp(m - m_new) * acc
                for j, pick, s in zip(blocks_of(it), picks, raws):
                    p = jnp.exp((s - m_new).astype(BF16))
                    acc = acc + _dot(v_rows(it, j), jnp.where(pick, p, jnp.zeros_like(p)))
                st[2 * it] = m_new
                st[2 * it + 1] = acc
            pipelined(work, past_scores(blocks_of), reduce)

        for qb in range(1, MOBA_QBLOCKS):
            attend(state, with_earlier(qb), earlier_in_step)

        def body(jj, st):
            st = list(st)
            attend(st, all_items, loop_blocks(jj))
            return tuple(st)

        state = lax.fori_loop(0, loop_trips, body, tuple(state))
        return tuple(state[1::2])

    def bounded_path():
        def attend(st, blocks_of):
            work = []
            for it in all_items:
                blocks = blocks_of(it)
                work += [(it, blocks[c:c + MOBA_ENTRY_BLOCKS]) for c in range(0, len(blocks), MOBA_ENTRY_BLOCKS)]

            def scores(entry):
                it, blocks = entry
                return [_dot(k_ref[0, pl.ds(block_offset(j), tq), cols(items[it][1])], qts[it]) for j, _ in blocks]

            def reduce(entry, raws):
                it, blocks = entry
                for (j, is_own), s in zip(blocks, raws):
                    p = jnp.exp(s).astype(BF16)
                    p = jnp.where(causal if is_own else picked(it, j), p, jnp.zeros_like(p))
                    term = _dot(v_rows(it, j), p)
                    st[it] = term if st[it] is None else st[it] + term

            pipelined(work, scores, reduce, MOBA_ENTRY_LOOKAHEAD)

        select_blocks()
        accs = [None] * n_items
        attend(accs, lambda it: [(j, False) for j in earlier_in_step(it)] + [(first + items[it][0], True)])

        def body(jj, st):
            st = list(st)
            attend(st, lambda it: [(j, False) for j in loop_blocks(jj)(it)])
            return tuple(st)

        return lax.fori_loop(0, loop_trips, body, tuple(accs))

    gq = gqk_ref[:, :MOBA_W]
    gk = gqk_ref[:, MOBA_W:]
    score_bound_sq = (HEAD_DIM * SCORE_BOUND_MARGIN) * jnp.max(gq * gq) * jnp.max(gk * gk)
    accs = lax.cond(score_bound_sq <= MOBA_SAFE_LOGIT * MOBA_SAFE_LOGIT, bounded_path, stable_path)
    for qb in range(MOBA_QBLOCKS):
        for pair in range(n_heads // 2):
            outs = []
            for h in (2 * pair, 2 * pair + 1):
                acc = accs[qb * n_heads + h]
                outs.append(acc[:HEAD_DIM] / acc[HEAD_DIM:HEAD_DIM + 1])
            o_ref[0, qb * tq:(qb + 1) * tq, pair * LANES:(pair + 1) * LANES] = (
                jnp.concatenate(outs, axis=0).T.astype(BF16))


def _moba_attention(qt, k, vt, kmean, gqk):
    b, s, w = k.shape
    nb = s // MOBA_BLOCK
    rows = MOBA_QBLOCKS * MOBA_BLOCK
    return pl.pallas_call(
        _moba_kernel,
        grid=(b, nb // MOBA_QBLOCKS),
        in_specs=[
            pl.BlockSpec((1, w, rows), lambda bi, i: (bi, 0, i)),
            pl.BlockSpec((1, s, w), lambda bi, i: (bi, 0, 0)),
            pl.BlockSpec((1, w // HEAD_DIM, VT_ROWS, s), lambda bi, i: (bi, 0, 0, 0)),
            pl.BlockSpec((1, nb, w), lambda bi, i: (bi, 0, 0)),
            pl.BlockSpec((1, 2 * w), lambda bi, i: (0, 0)),
        ],
        out_specs=pl.BlockSpec((1, rows, w), lambda bi, i: (bi, i, 0)),
        out_shape=jax.ShapeDtypeStruct((b, s, w), BF16),
        compiler_params=pltpu.CompilerParams(dimension_semantics=("arbitrary",) * 2, vmem_limit_bytes=VMEM_LIMIT),
        name="moba_attention",
    )(qt, k, vt, kmean, gqk)


def _memattn_kernel(q_ref, ck_ref, cvt_ref, o_ref):
    tq = MXU_TILE
    lane = lax.broadcasted_iota(jnp.int32, (tq, LANES), 1)
    low = lane < HEAD_DIM
    work = [(tile, hd) for tile in range(q_ref.shape[1] // tq) for hd in range(MEM_HEADS)]

    def cols(hd):
        return slice((hd // 2) * LANES, (hd // 2 + 1) * LANES)

    def scores(w):
        tile, hd = work[w]
        q2 = q_ref[0, tile * tq:(tile + 1) * tq, cols(hd)]
        zero = jnp.zeros_like(q2)
        qm = jnp.where(low, q2, zero) if hd % 2 == 0 else jnp.where(low, zero, q2)
        return _dot_nt(ck_ref[0, :, cols(hd)], qm)

    queued = {w: scores(w) for w in range(MEMATTN_LOOKAHEAD)}
    outs = {}
    for w, (tile, hd) in enumerate(work):
        if w + MEMATTN_LOOKAHEAD < len(work):
            queued[w + MEMATTN_LOOKAHEAD] = scores(w + MEMATTN_LOOKAHEAD)
        s = queued.pop(w)
        p = jnp.exp(s - jnp.max(s, axis=0, keepdims=True)).astype(BF16)
        acc = _dot(cvt_ref[0, hd], p)
        outs[hd % 2] = acc[:HEAD_DIM] / acc[HEAD_DIM:HEAD_DIM + 1]
        if hd % 2 == 1:
            o_ref[0, tile * tq:(tile + 1) * tq, cols(hd)] = (
                jnp.concatenate([outs[0], outs[1]], axis=0).T.astype(BF16))


def _mem_attention(q, ck, cv):
    b, s, w = q.shape
    m = ck.shape[1]
    tm = MEMATTN_ROWS
    return pl.pallas_call(
        _memattn_kernel,
        grid=(b, s // tm),
        in_specs=[
            pl.BlockSpec((1, tm, w), lambda bi, i: (bi, i, 0)),
            pl.BlockSpec((1, m, w), lambda bi, i: (bi, 0, 0)),
            pl.BlockSpec((1, MEM_HEADS, VT_ROWS, m), lambda bi, i: (bi, 0, 0, 0)),
        ],
        out_specs=pl.BlockSpec((1, tm, w), lambda bi, i: (bi, i, 0)),
        out_shape=jax.ShapeDtypeStruct((b, s, w), BF16),
        compiler_params=pltpu.CompilerParams(dimension_semantics=("arbitrary", "arbitrary"),
                                             vmem_limit_bytes=VMEM_LIMIT),
        name="mem_attention",
    )(q, ck, cv)


def _sigmoid(t):
    return 1.0 / (1.0 + jnp.exp(-t))


def _tail_kernel(x_ref, osb_ref, omo_ref, ome_ref, gmix_ref, gffn_ref, wg_ref, wsb_ref, wmo_ref, wme_ref,
                 wout_ref, wfi_ref, wfd_ref, out_ref):
    x = x_ref[...]
    d = x.shape[1]
    h = _rms_rows(x, gmix_ref[...]).astype(BF16)
    mix = None
    for n, (o_ref, wu_ref) in enumerate(((osb_ref, wsb_ref), (omo_ref, wmo_ref), (ome_ref, wme_ref))):
        gate = _dot(h, wg_ref[:, n * d:(n + 1) * d])
        term = _sigmoid(gate) * _dot(o_ref[...], wu_ref[...])
        mix = term if mix is None else mix + term
    x1 = x + _dot(mix.astype(BF16), wout_ref[...])
    h2 = _rms_rows(x1, gffn_ref[...]).astype(BF16)
    dff = wfd_ref.shape[0]
    gate = _dot(h2, wfi_ref[:, :dff])
    up = _dot(h2, wfi_ref[:, dff:])
    ff = (gate * _sigmoid(gate) * up).astype(BF16)
    out_ref[...] = x1 + _dot(ff, wfd_ref[...])


def _tail(x2, osb, omo, ome, gmix, gffn, wg, wsb, wmo, wme, wout, wfi, wfd):
    n, d = x2.shape
    tm = TAIL_ROWS
    rows = lambda i: (i, 0)

    def resident(arr):
        return pl.BlockSpec(arr.shape, lambda i: (0, 0), pipeline_mode=pl.Buffered(1))

    return pl.pallas_call(
        _tail_kernel,
        grid=(n // tm,),
        in_specs=[
            pl.BlockSpec((tm, d), rows),
            pl.BlockSpec((tm, SB_W), rows),
            pl.BlockSpec((tm, MOBA_W), rows),
            pl.BlockSpec((tm, MEM_W), rows),
            resident(gmix), resident(gffn), resident(wg), resident(wsb), resident(wmo), resident(wme),
            resident(wout), resident(wfi), resident(wfd),
        ],
        out_specs=pl.BlockSpec((tm, d), rows),
        out_shape=jax.ShapeDtypeStruct((n, d), F32),
        compiler_params=pltpu.CompilerParams(dimension_semantics=("arbitrary",), vmem_limit_bytes=VMEM_LIMIT),
        name="merge_out_ffn",
    )(x2, osb, omo, ome, gmix, gffn, wg, wsb, wmo, wme, wout, wfi, wfd)


def _rope_tables(seq):
    half = HEAD_DIM // 2
    inv_freq = ROPE_THETA ** (-jnp.arange(half, dtype=F32) * 2.0 / HEAD_DIM)
    ang = jnp.arange(seq, dtype=F32)[:, None] * inv_freq[None, :]
    cos, sin = jnp.cos(ang), jnp.sin(ang)
    reps = LANES // HEAD_DIM
    return jnp.tile(jnp.concatenate([cos, cos], axis=1), (1, reps)), jnp.tile(jnp.concatenate([-sin, sin], axis=1), (1, reps))


def _head_mean_matrix(width):
    r = jnp.arange(width) // HEAD_DIM
    return jnp.where(r[:, None] == r[None, :], 1.0 / HEAD_DIM, 0.0).astype(BF16)


def _layer(x, mem, mix_norm_g, mem_norm_g, ffn_norm_g, w_in, w_mem_kv, moba_q_norm_g, moba_k_norm_g,
           mem_q_norm_g, mem_k_norm_g, w_up_sb, w_up_moba, w_up_mem, w_out, w_ffn_in, w_ffn_down):
    b, s, d = x.shape
    x2 = x.reshape(b * s, d)
    cos, sin = _rope_tables(s)
    bd = _head_mean_matrix(MXU_TILE)
    row = lambda g, reps=1: jnp.tile(g.astype(F32), reps)[None, :]
    gqk = jnp.concatenate([row(moba_q_norm_g, MOBA_HEADS), row(moba_k_norm_g, MOBA_HEADS)], axis=1)

    ck, cv = _memkv(mem, row(mem_norm_g), w_mem_kv.astype(BF16), bd, row(mem_k_norm_g, MEM_HEADS))
    sbq, sbk, sbv, moqt, mok, movt, meq, kmean = _inproj(
        x2, row(mix_norm_g), _split_projection(w_in[:, :QKV_W].astype(BF16)), cos, sin, bd, gqk,
        row(mem_q_norm_g, MEM_HEADS), s)
    seq3 = lambda t: t.reshape(b, s, t.shape[-1])
    o_sb = _sb_attention(seq3(sbq), seq3(sbk), seq3(sbv))
    o_mo = _moba_attention(moqt, seq3(mok), movt, kmean.reshape(b, s // MOBA_BLOCK, MOBA_W), gqk)
    o_me = _mem_attention(seq3(meq), ck, cv)
    flat = lambda t: t.reshape(b * s, t.shape[-1])
    out = _tail(x2, flat(o_sb), flat(o_mo), flat(o_me), row(mix_norm_g), row(ffn_norm_g),
                w_in[:, QKV_W:].astype(BF16), w_up_sb.astype(BF16), w_up_moba.astype(BF16), w_up_mem.astype(BF16),
                w_out.astype(BF16), w_ffn_in.astype(BF16), w_ffn_down.astype(BF16))
    return out.reshape(b, s, d)


def kernel(x, mem, mix_norm_g, mem_norm_g, ffn_norm_g, w_in, w_mem_kv, moba_q_norm_g, moba_k_norm_g, mem_q_norm_g, mem_k_norm_g, w_up_sb, w_up_moba, w_up_mem, w_out, w_ffn_in, w_ffn_down):
    for l in range(w_in.shape[0]):
        x = _layer(x, mem, mix_norm_g[l], mem_norm_g[l], ffn_norm_g[l], w_in[l], w_mem_kv[l],
                   moba_q_norm_g[l], moba_k_norm_g[l], mem_q_norm_g[l], mem_k_norm_g[l],
                   w_up_sb[l], w_up_moba[l], w_up_mem[l], w_out[l], w_ffn_in[l], w_ffn_down[l])
    return x
```

```python
import functools
import math

import jax
import jax.numpy as jnp
from jax import lax
from jax.experimental import pallas as pl
from jax.experimental.pallas import tpu as pltpu

F32 = jnp.float32
BF16 = jnp.bfloat16

HEAD_DIM = 64
SB_HEADS = 6
MOBA_HEADS = 6
MEM_HEADS = 4
SB_W = SB_HEADS * HEAD_DIM
MOBA_W = MOBA_HEADS * HEAD_DIM
MEM_W = MEM_HEADS * HEAD_DIM
QKV_W = 3 * SB_W + 3 * MOBA_W + MEM_W
MOBA_BLOCK = 256
MOBA_TOPK = 3
ROPE_THETA = 10000.0
EPS = 1e-6
QK_SCALE = 1.0 / math.sqrt(HEAD_DIM)

LANES = 128
MXU_TILE = 256
BF16_SUBLANES = 16
VT_ROWS = HEAD_DIM + BF16_SUBLANES
NEG_BIG = -1e30

SB_DONE_LOG = -30.0

INPROJ_ROWS = 1024
SB_ROWS = 1024
SB_SUB = 64
SB_WIN = 128
SB_LOOKAHEAD = 10
MOBA_LOOKAHEAD = 2
MOBA_QBLOCKS = 4
MOBA_LOOP_BLOCKS = 4
MOBA_ENTRY_BLOCKS = 2
MOBA_ENTRY_LOOKAHEAD = 3
MOBA_SAFE_LOGIT = 40.0
SCORE_BOUND_MARGIN = 1.1
MEMATTN_ROWS = 1024
MEMATTN_LOOKAHEAD = 5
TAIL_ROWS = 512
VMEM_LIMIT = 56 * 1024 * 1024


def _dot(a, b):
    return jnp.dot(a, b, preferred_element_type=F32)


def _dot_nt(a, b):
    return lax.dot_general(a, b, (((1,), (1,)), ((), ())), preferred_element_type=F32)


def _split_bf16(t):
    hi = t.astype(BF16)
    lo = (t - hi.astype(F32)).astype(BF16)
    return hi, lo


def _rms_rows(x, g):
    ms = jnp.mean(x * x, axis=-1, keepdims=True)
    return x * lax.rsqrt(ms + EPS) * g


def _head_rms(t, bd, g):
    hi, lo = _split_bf16(t * t)
    parts = []
    for c in range(0, t.shape[1], MXU_TILE):
        w = min(MXU_TILE, t.shape[1] - c)
        parts.append(_dot(hi[:, c:c + w], bd[:w, :w]) + _dot(lo[:, c:c + w], bd[:w, :w]))
    ms = parts[0] if len(parts) == 1 else jnp.concatenate(parts, axis=1)
    return t * lax.rsqrt(ms + EPS) * g


def _rotate_half_pairs(t):
    lane = lax.broadcasted_iota(jnp.int32, t.shape, 1)
    first_half = (lane & (HEAD_DIM - 1)) < (HEAD_DIM // 2)
    return jnp.where(first_half, pltpu.roll(t, LANES - HEAD_DIM // 2, 1), pltpu.roll(t, HEAD_DIM // 2, 1))


def _rope(t, cos, sin_signed):
    outs = []
    for c in range(t.shape[1] // LANES):
        tc = t[:, c * LANES:(c + 1) * LANES]
        outs.append(tc * cos + _rotate_half_pairs(tc) * sin_signed)
    return jnp.concatenate(outs, axis=1)


def _memkv_kernel(mem_ref, g_ref, w_ref, bd_ref, gk_ref, ck_ref, cvt_ref):
    h = _rms_rows(mem_ref[0], g_ref[...]).astype(BF16)
    kv = _dot(h, w_ref[...])
    k = kv[:, :MEM_W]
    ck_ref[0] = _head_rms(k, bd_ref[...], gk_ref[...]).astype(BF16)
    vt = kv[:, MEM_W:].T
    for hd in range(MEM_HEADS):
        cvt_ref[0, hd, 0:HEAD_DIM, :] = vt[hd * HEAD_DIM:(hd + 1) * HEAD_DIM, :].astype(BF16)
        cvt_ref[0, hd, HEAD_DIM:, :] = jnp.ones((VT_ROWS - HEAD_DIM, vt.shape[1]), BF16)


def _memkv(mem, g, w, bd, gk):
    b, m, d = mem.shape
    return pl.pallas_call(
        _memkv_kernel,
        grid=(b,),
        in_specs=[
            pl.BlockSpec((1, m, d), lambda i: (i, 0, 0)),
            pl.BlockSpec((1, d), lambda i: (0, 0)),
            pl.BlockSpec((d, 2 * MEM_W), lambda i: (0, 0)),
            pl.BlockSpec((MXU_TILE, MXU_TILE), lambda i: (0, 0)),
            pl.BlockSpec((1, MEM_W), lambda i: (0, 0)),
        ],
        out_specs=[
            pl.BlockSpec((1, m, MEM_W), lambda i: (i, 0, 0)),
            pl.BlockSpec((1, MEM_HEADS, VT_ROWS, m), lambda i: (i, 0, 0, 0)),
        ],
        out_shape=[jax.ShapeDtypeStruct((b, m, MEM_W), BF16),
                   jax.ShapeDtypeStruct((b, MEM_HEADS, VT_ROWS, m), BF16)],
        compiler_params=pltpu.CompilerParams(dimension_semantics=("arbitrary",), vmem_limit_bytes=VMEM_LIMIT),
        name="memkv",
    )(mem, g, w, bd, gk)


def _inproj_kernel(x_ref, g_ref, wa_ref, wb_ref, wc_ref, cos_ref, sin_ref, bd_ref, gqk_ref, gmq_ref,
                   sbq_ref, sbk_ref, sbv_ref, moqt_ref, mok_ref, movt_ref, meq_ref, kmean_ref):
    h = _rms_rows(x_ref[...], g_ref[...]).astype(BF16)
    pa = _dot(h, wa_ref[...])
    pb = _dot(h, wb_ref[...])
    bd = bd_ref[...]
    qk_normed = _head_rms(pa[:, :2 * MOBA_W], bd, gqk_ref[...])
    cq = _head_rms(pa[:, 3 * MOBA_W:3 * MOBA_W + MEM_W], bd, gmq_ref[...])
    pc = _dot(h, wc_ref[...])

    qk = _rope(qk_normed, cos_ref[...], sin_ref[...])
    moqt_ref[0] = (qk[:, :MOBA_W] * QK_SCALE).T.astype(BF16)
    mk = qk[:, MOBA_W:]
    mok_ref[...] = mk.astype(BF16)
    for blk in range(INPROJ_ROWS // MOBA_BLOCK):
        kmean_ref[0, blk:blk + 1, :] = jnp.mean(mk[blk * MOBA_BLOCK:(blk + 1) * MOBA_BLOCK], axis=0, keepdims=True)
    vt = pa[:, 2 * MOBA_W:3 * MOBA_W].T
    for hd in range(MOBA_HEADS):
        movt_ref[0, hd, 0:HEAD_DIM, :] = vt[hd * HEAD_DIM:(hd + 1) * HEAD_DIM, :].astype(BF16)
        movt_ref[0, hd, HEAD_DIM:, :] = jnp.ones((VT_ROWS - HEAD_DIM, vt.shape[1]), BF16)
    meq_ref[...] = (cq * QK_SCALE).astype(BF16)

    sb = jnp.concatenate([pa[:, 3 * MOBA_W + MEM_W:], pb, pc], axis=1)
    sbq_ref[...] = (sb[:, :SB_W] * QK_SCALE).astype(BF16)
    sbk_ref[...] = sb[:, SB_W:2 * SB_W].astype(BF16)
    sbv_ref[...] = sb[:, 2 * SB_W:].astype(BF16)


def _split_projection(w):
    sb = w[:, :3 * SB_W]
    rest = w[:, 3 * SB_W:]
    head = -rest.shape[1] % MXU_TILE
    mid = head + (sb.shape[1] - head) // (2 * MXU_TILE) * MXU_TILE
    return jnp.concatenate([rest, sb[:, :head]], axis=1), sb[:, head:mid], sb[:, mid:]


def _inproj(x2, g, ws, cos, sin, bd, gqk, gmq, seq):
    n, d = x2.shape
    tm = INPROJ_ROWS
    steps = n // tm
    per_seq = seq // tm
    blocks_per_step = tm // MOBA_BLOCK
    const = lambda i: (0, 0)
    rows = lambda i: (i, 0)
    out_shapes = [jax.ShapeDtypeStruct((n, SB_W), BF16)] * 3 + [
        jax.ShapeDtypeStruct((n // seq, MOBA_W, seq), BF16),
        jax.ShapeDtypeStruct((n, MOBA_W), BF16),
        jax.ShapeDtypeStruct((n // seq, MOBA_HEADS, VT_ROWS, seq), BF16),
        jax.ShapeDtypeStruct((n, MEM_W), BF16),
        jax.ShapeDtypeStruct((steps, blocks_per_step, MOBA_W), F32),
    ]
    out_specs = [pl.BlockSpec((tm, SB_W), rows)] * 3 + [
        pl.BlockSpec((1, MOBA_W, tm), lambda i: (i // per_seq, 0, i % per_seq)),
        pl.BlockSpec((tm, MOBA_W), rows),
        pl.BlockSpec((1, MOBA_HEADS, VT_ROWS, tm), lambda i: (i // per_seq, 0, 0, i % per_seq)),
        pl.BlockSpec((tm, MEM_W), rows),
        pl.BlockSpec((1, blocks_per_step, MOBA_W), lambda i: (i, 0, 0)),
    ]
    return pl.pallas_call(
        _inproj_kernel,
        grid=(steps,),
        in_specs=[
            pl.BlockSpec((tm, d), rows),
            pl.BlockSpec((1, d), const),
            *[pl.BlockSpec(w.shape, const) for w in ws],
            pl.BlockSpec((tm, LANES), lambda i: (i % per_seq, 0)),
            pl.BlockSpec((tm, LANES), lambda i: (i % per_seq, 0)),
            pl.BlockSpec((MXU_TILE, MXU_TILE), const),
            pl.BlockSpec((1, 2 * MOBA_W), const),
            pl.BlockSpec((1, MEM_W), const),
        ],
        out_specs=out_specs,
        out_shape=out_shapes,
        compiler_params=pltpu.CompilerParams(dimension_semantics=("arbitrary",), vmem_limit_bytes=VMEM_LIMIT),
        name="inproj",
    )(x2, g, *ws, cos, sin, bd, gqk, gmq)


def _sb_kernel(q_ref, k_ref, v_ref, o_ref):
    i = pl.program_id(1)
    lane = lax.broadcasted_iota(jnp.int32, (SB_SUB, LANES), 1)
    low = lane < HEAD_DIM
    kiota = lax.broadcasted_iota(jnp.int32, (SB_SUB, SB_WIN), 1)
    riota = lax.broadcasted_iota(jnp.int32, (SB_SUB, 1), 0)
    ur = lax.broadcasted_iota(jnp.int32, (SB_WIN, SB_WIN), 0)
    uc = lax.broadcasted_iota(jnp.int32, (SB_WIN, SB_WIN), 1)
    later = (ur > uc).astype(BF16)
    n_pairs = SB_W // LANES

    n_sub = SB_ROWS // SB_SUB
    chains = []
    for sub in range(n_sub):
        for pair in range(n_pairs):
            q2 = q_ref[0, sub * SB_SUB:(sub + 1) * SB_SUB, pair * LANES:(pair + 1) * LANES]
            zero = jnp.zeros_like(q2)
            chains += [(sub, pair, jnp.where(low, q2, zero)), (sub, pair, jnp.where(low, zero, q2))]
    n_chains = len(chains)

    def scan_windows(starts, limits, carries):
        def cols(c):
            return slice(chains[c][1] * LANES, (chains[c][1] + 1) * LANES)

        def scores(c):
            sub = chains[c][0]
            return _dot_nt(chains[c][2], k_ref[0, pl.ds(starts[sub], SB_WIN), cols(c)])

        valids = [(starts[sub] + kiota) < limits[sub] for sub in range(n_sub)]

        def scan(c, z):
            valid = valids[chains[c][0]]
            log_beta = jnp.minimum(z, 0.0) - jnp.log(1.0 + jnp.exp(-jnp.abs(z)))
            log_1m = jnp.where(valid, log_beta - z, 0.0)
            hi, lo = _split_bf16(log_1m)
            tail = _dot(hi, later) + _dot(lo, later)
            return log_beta, tail, carries[c] + jnp.sum(log_1m, axis=1, keepdims=True)

        def weigh(c, log_beta, tail):
            sub = chains[c][0]
            a = jnp.where(valids[sub], jnp.exp(log_beta + tail + carries[c]), 0.0)
            return _dot(a.astype(BF16), v_ref[0, pl.ds(starts[sub], SB_WIN), cols(c)])

        zs, scans, outs = {}, {}, [None] * n_chains
        for t in range(n_chains + 2 * SB_LOOKAHEAD):
            if t < n_chains:
                zs[t] = scores(t)
            c = t - SB_LOOKAHEAD
            if 0 <= c < n_chains:
                scans[c] = scan(c, zs.pop(c))
            c = t - 2 * SB_LOOKAHEAD
            if 0 <= c < n_chains:
                log_beta, tail, carry = scans.pop(c)
                outs[c] = (weigh(c, log_beta, tail), carry)
        return outs

    def unfinished(cs, starts):
        worst = None
        for c, carry in enumerate(cs):
            open_carry = jnp.where(starts[chains[c][0]] > 0, carry, NEG_BIG)
            worst = open_carry if worst is None else jnp.maximum(worst, open_carry)
        return (jnp.max(worst) > SB_DONE_LOG).astype(jnp.int32)

    t0s = [i * SB_ROWS + sub * SB_SUB for sub in range(n_sub)]
    zero_carries = [jnp.zeros((SB_SUB, 1), F32)] * n_chains

    def window_before(ends):
        starts = [pl.multiple_of(jnp.maximum(e - SB_WIN, 0), SB_SUB) for e in ends]
        limits = [jnp.minimum(t0 + riota, e) for t0, e in zip(t0s, ends)]
        return starts, limits

    def write(accs):
        for sub in range(n_sub):
            for pair in range(n_pairs):
                c = (sub * n_pairs + pair) * 2
                out = jnp.where(low, accs[c], accs[c + 1])
                o_ref[0, sub * SB_SUB:(sub + 1) * SB_SUB, pair * LANES:(pair + 1) * LANES] = out.astype(BF16)

    ends0 = [t0 + SB_SUB for t0 in t0s]
    starts0, limits0 = window_before(ends0)
    first = scan_windows(starts0, limits0, zero_carries)
    write([acc for acc, _ in first])
    @pl.when(unfinished([carry for _, carry in first], starts0) > 0)
    def _():
        def cond(state):
            ends, more = state[:n_sub], state[n_sub]
            return jnp.logical_and(ends[-1] > 0, more > 0)

        def body(state):
            ends = state[:n_sub]
            accs_c = state[n_sub + 1:n_sub + 1 + n_chains]
            cs_c = state[n_sub + 1 + n_chains:]
            starts, limits = window_before(ends)
            res = scan_windows(starts, limits, list(cs_c))
            new_cs = [r[1] for r in res]
            return (*starts, unfinished(new_cs, starts), *[a + r[0] for a, r in zip(accs_c, res)], *new_cs)

        zero_accs = [jnp.zeros((SB_SUB, LANES), F32)] * n_chains
        state = lax.while_loop(cond, body, (*ends0, jnp.int32(1), *zero_accs, *zero_carries))
        write(state[n_sub + 1:n_sub + 1 + n_chains])


def _sb_attention(q, k, v):
    b, s, w = q.shape
    return pl.pallas_call(
        _sb_kernel,
        grid=(b, s // SB_ROWS),
        in_specs=[
            pl.BlockSpec((1, SB_ROWS, w), lambda bi, i: (bi, i, 0)),
            pl.BlockSpec((1, s, w), lambda bi, i: (bi, 0, 0)),
            pl.BlockSpec((1, s, w), lambda bi, i: (bi, 0, 0)),
        ],
        out_specs=pl.BlockSpec((1, SB_ROWS, w), lambda bi, i: (bi, i, 0)),
        out_shape=jax.ShapeDtypeStruct((b, s, w), BF16),
        compiler_params=pltpu.CompilerParams(dimension_semantics=("arbitrary", "arbitrary"),
                                             vmem_limit_bytes=VMEM_LIMIT),
        name="sb_attention",
    )(q, k, v)


def _moba_kernel(qt_ref, k_ref, vt_ref, km_ref, gqk_ref, o_ref):
    first = pl.program_id(1) * MOBA_QBLOCKS
    nb = km_ref.shape[1]
    tq = MOBA_BLOCK
    n_heads = vt_ref.shape[1]
    blk = lax.broadcasted_iota(jnp.int32, (nb, tq), 0)
    items = [(qb, h) for qb in range(MOBA_QBLOCKS) for h in range(n_heads)]

    def cols(h):
        pair = h // 2
        return slice(pair * LANES, (pair + 1) * LANES)

    first_head_rows = lax.broadcasted_iota(jnp.int32, (LANES, MOBA_QBLOCKS * tq), 0) < HEAD_DIM
    qt_heads = []
    for pair in range(n_heads // 2):
        q2 = qt_ref[0, pair * LANES:(pair + 1) * LANES, :]
        zero = jnp.zeros_like(q2)
        qt_heads += [jnp.where(first_head_rows, q2, zero), jnp.where(first_head_rows, zero, q2)]
    qts = [qt_heads[h][:, qb * tq:(qb + 1) * tq] for qb, h in items]

    bits = [None] * len(items)

    def select_blocks():
        km_split = [_split_bf16(km_ref[0, :, cols(h)]) for h in range(n_heads)]
        gates = [_dot(km_split[h][0], qts[it]) + _dot(km_split[h][1], qts[it])
                 for it, (_, h) in enumerate(items)]
        blkf = blk.astype(F32)
        for it, (qb, _) in enumerate(items):
            gate = jnp.where(blk < first + qb, gates[it], -jnp.inf)
            chosen = jnp.zeros((1, tq), jnp.int32)
            for _ in range(MOBA_TOPK):
                best = jnp.max(gate, axis=0, keepdims=True)
                idx = jnp.min(jnp.where(gate == best, blkf, float(nb)), axis=0, keepdims=True)
                chosen = chosen | jnp.where(best > -jnp.inf, jnp.left_shift(1, idx.astype(jnp.int32)), 0)
                gate = jnp.where(blkf == idx, -jnp.inf, gate)
            bits[it] = chosen

    def block_offset(j):
        return pl.multiple_of(j * tq, tq)

    def pipelined(work, scores, reduce, lookahead=MOBA_LOOKAHEAD):
        queued = {n: scores(w) for n, w in enumerate(work[:lookahead])}
        for n, w in enumerate(work):
            if n + lookahead < len(work):
                queued[n + lookahead] = scores(work[n + lookahead])
            reduce(w, queued.pop(n))

    causal = lax.broadcasted_iota(jnp.int32, (tq, tq), 0) <= lax.broadcasted_iota(jnp.int32, (tq, tq), 1)
    n_items = len(items)
    all_items = list(range(n_items))

    def own_scores(it):
        qb, h = items[it]
        return _dot(k_ref[0, pl.ds(block_offset(first + qb), tq), cols(h)], qts[it])

    def past_scores(blocks_of):
        def scores(it):
            h = items[it][1]
            return [_dot(k_ref[0, pl.ds(block_offset(j), tq), cols(h)], qts[it]) for j in blocks_of(it)]
        return scores

    def v_rows(it, j):
        return vt_ref[0, items[it][1], :, pl.ds(block_offset(j), tq)]

    def picked(it, j):
        return (jnp.right_shift(bits[it], j) & 1) > 0

    def earlier_in_step(it):
        return [first + e for e in range(items[it][0])]

    def loop_blocks(jj):
        return lambda it: [MOBA_LOOP_BLOCKS * jj + u for u in range(MOBA_LOOP_BLOCKS)]

    loop_trips = first // MOBA_LOOP_BLOCKS

    def with_earlier(qb):
        return [it for it in all_items if items[it][0] == qb]

    def stable_path():
        state = [None] * (2 * n_items)
        select_blocks()

        def own_reduce(it, raw):
            s = jnp.where(causal, raw, NEG_BIG)
            m = jnp.max(s, axis=0, keepdims=True)
            state[2 * it] = m
            state[2 * it + 1] = _dot(v_rows(it, first + items[it][0]), jnp.exp(s - m).astype(BF16))

        pipelined(all_items, own_scores, own_reduce)

        def attend(st, work, blocks_of):
            def reduce(it, raws):
                m, acc = st[2 * it], st[2 * it + 1]
                picks = [picked(it, j) for j in blocks_of(it)]
                m_new = m
                for pick, s in zip(picks, raws):
                    m_new = jnp.maximum(m_new, jnp.where(pick, jnp.max(s, axis=0, keepdims=True), NEG_BIG))
                acc = jnp.exp(m - m_new) * acc
                for j, pick, s in zip(blocks_of(it), picks, raws):
                    p = jnp.exp((s - m_new).astype(BF16))
                    acc = acc + _dot(v_rows(it, j), jnp.where(pick, p, jnp.zeros_like(p)))
                st[2 * it] = m_new
                st[2 * it + 1] = acc
            pipelined(work, past_scores(blocks_of), reduce)

        for qb in range(1, MOBA_QBLOCKS):
            attend(state, with_earlier(qb), earlier_in_step)

        def body(jj, st):
            st = list(st)
            attend(st, all_items, loop_blocks(jj))
            return tuple(st)

        state = lax.fori_loop(0, loop_trips, body, tuple(state))
        return tuple(state[1::2])

    def bounded_path():
        def attend(st, blocks_of):
            work = []
            for it in all_items:
                blocks = blocks_of(it)
                work += [(it, blocks[c:c + MOBA_ENTRY_BLOCKS]) for c in range(0, len(blocks), MOBA_ENTRY_BLOCKS)]

            def scores(entry):
                it, blocks = entry
                return [_dot(k_ref[0, pl.ds(block_offset(j), tq), cols(items[it][1])], qts[it]) for j, _ in blocks]

            def reduce(entry, raws):
                it, blocks = entry
                for (j, is_own), s in zip(blocks, raws):
                    p = jnp.exp(s).astype(BF16)
                    p = jnp.where(causal if is_own else picked(it, j), p, jnp.zeros_like(p))
                    term = _dot(v_rows(it, j), p)
                    st[it] = term if st[it] is None else st[it] + term

            pipelined(work, scores, reduce, MOBA_ENTRY_LOOKAHEAD)

        select_blocks()
        accs = [None] * n_items
        attend(accs, lambda it: [(j, False) for j in earlier_in_step(it)] + [(first + items[it][0], True)])

        def body(jj, st):
            st = list(st)
            attend(st, lambda it: [(j, False) for j in loop_blocks(jj)(it)])
            return tuple(st)

        return lax.fori_loop(0, loop_trips, body, tuple(accs))

    gq = gqk_ref[:, :MOBA_W]
    gk = gqk_ref[:, MOBA_W:]
    score_bound_sq = (HEAD_DIM * SCORE_BOUND_MARGIN) * jnp.max(gq * gq) * jnp.max(gk * gk)
    accs = lax.cond(score_bound_sq <= MOBA_SAFE_LOGIT * MOBA_SAFE_LOGIT, bounded_path, stable_path)
    for qb in range(MOBA_QBLOCKS):
        for pair in range(n_heads // 2):
            outs = []
            for h in (2 * pair, 2 * pair + 1):
                acc = accs[qb * n_heads + h]
                outs.append(acc[:HEAD_DIM] / acc[HEAD_DIM:HEAD_DIM + 1])
            o_ref[0, qb * tq:(qb + 1) * tq, pair * LANES:(pair + 1) * LANES] = (
                jnp.concatenate(outs, axis=0).T.astype(BF16))


def _moba_attention(qt, k, vt, kmean, gqk):
    b, s, w = k.shape
    nb = s // MOBA_BLOCK
    rows = MOBA_QBLOCKS * MOBA_BLOCK
    return pl.pallas_call(
        _moba_kernel,
        grid=(b, nb // MOBA_QBLOCKS),
        in_specs=[
            pl.BlockSpec((1, w, rows), lambda bi, i: (bi, 0, i)),
            pl.BlockSpec((1, s, w), lambda bi, i: (bi, 0, 0)),
            pl.BlockSpec((1, w // HEAD_DIM, VT_ROWS, s), lambda bi, i: (bi, 0, 0, 0)),
            pl.BlockSpec((1, nb, w), lambda bi, i: (bi, 0, 0)),
            pl.BlockSpec((1, 2 * w), lambda bi, i: (0, 0)),
        ],
        out_specs=pl.BlockSpec((1, rows, w), lambda bi, i: (bi, i, 0)),
        out_shape=jax.ShapeDtypeStruct((b, s, w), BF16),
        compiler_params=pltpu.CompilerParams(dimension_semantics=("arbitrary",) * 2, vmem_limit_bytes=VMEM_LIMIT),
        name="moba_attention",
    )(qt, k, vt, kmean, gqk)


def _memattn_kernel(q_ref, ck_ref, cvt_ref, o_ref):
    tq = MXU_TILE
    lane = lax.broadcasted_iota(jnp.int32, (tq, LANES), 1)
    low = lane < HEAD_DIM
    work = [(tile, hd) for tile in range(q_ref.shape[1] // tq) for hd in range(MEM_HEADS)]

    def cols(hd):
        return slice((hd // 2) * LANES, (hd // 2 + 1) * LANES)

    def scores(w):
        tile, hd = work[w]
        q2 = q_ref[0, tile * tq:(tile + 1) * tq, cols(hd)]
        zero = jnp.zeros_like(q2)
        qm = jnp.where(low, q2, zero) if hd % 2 == 0 else jnp.where(low, zero, q2)
        return _dot_nt(ck_ref[0, :, cols(hd)], qm)

    queued = {w: scores(w) for w in range(MEMATTN_LOOKAHEAD)}
    outs = {}
    for w, (tile, hd) in enumerate(work):
        if w + MEMATTN_LOOKAHEAD < len(work):
            queued[w + MEMATTN_LOOKAHEAD] = scores(w + MEMATTN_LOOKAHEAD)
        s = queued.pop(w)
        p = jnp.exp(s - jnp.max(s, axis=0, keepdims=True)).astype(BF16)
        acc = _dot(cvt_ref[0, hd], p)
        outs[hd % 2] = acc[:HEAD_DIM] / acc[HEAD_DIM:HEAD_DIM + 1]
        if hd % 2 == 1:
            o_ref[0, tile * tq:(tile + 1) * tq, cols(hd)] = (
                jnp.concatenate([outs[0], outs[1]], axis=0).T.astype(BF16))


def _mem_attention(q, ck, cv):
    b, s, w = q.shape
    m = ck.shape[1]
    tm = MEMATTN_ROWS
    return pl.pallas_call(
        _memattn_kernel,
        grid=(b, s // tm),
        in_specs=[
            pl.BlockSpec((1, tm, w), lambda bi, i: (bi, i, 0)),
            pl.BlockSpec((1, m, w), lambda bi, i: (bi, 0, 0)),
            pl.BlockSpec((1, MEM_HEADS, VT_ROWS, m), lambda bi, i: (bi, 0, 0, 0)),
        ],
        out_specs=pl.BlockSpec((1, tm, w), lambda bi, i: (bi, i, 0)),
        out_shape=jax.ShapeDtypeStruct((b, s, w), BF16),
        compiler_params=pltpu.CompilerParams(dimension_semantics=("arbitrary", "arbitrary"),
                                             vmem_limit_bytes=VMEM_LIMIT),
        name="mem_attention",
    )(q, ck, cv)


def _sigmoid(t):
    return 1.0 / (1.0 + jnp.exp(-t))


def _tail_kernel(x_ref, osb_ref, omo_ref, ome_ref, gmix_ref, gffn_ref, wg_ref, wsb_ref, wmo_ref, wme_ref,
                 wout_ref, wfi_ref, wfd_ref, out_ref):
    x = x_ref[...]
    d = x.shape[1]
    h = _rms_rows(x, gmix_ref[...]).astype(BF16)
    mix = None
    for n, (o_ref, wu_ref) in enumerate(((osb_ref, wsb_ref), (omo_ref, wmo_ref), (ome_ref, wme_ref))):
        gate = _dot(h, wg_ref[:, n * d:(n + 1) * d])
        term = _sigmoid(gate) * _dot(o_ref[...], wu_ref[...])
        mix = term if mix is None else mix + term
    x1 = x + _dot(mix.astype(BF16), wout_ref[...])
    h2 = _rms_rows(x1, gffn_ref[...]).astype(BF16)
    dff = wfd_ref.shape[0]
    gate = _dot(h2, wfi_ref[:, :dff])
    up = _dot(h2, wfi_ref[:, dff:])
    ff = (gate * _sigmoid(gate) * up).astype(BF16)
    out_ref[...] = x1 + _dot(ff, wfd_ref[...])


def _tail(x2, osb, omo, ome, gmix, gffn, wg, wsb, wmo, wme, wout, wfi, wfd):
    n, d = x2.shape
    tm = TAIL_ROWS
    rows = lambda i: (i, 0)

    def resident(arr):
        return pl.BlockSpec(arr.shape, lambda i: (0, 0), pipeline_mode=pl.Buffered(1))

    return pl.pallas_call(
        _tail_kernel,
        grid=(n // tm,),
        in_specs=[
            pl.BlockSpec((tm, d), rows),
            pl.BlockSpec((tm, SB_W), rows),
            pl.BlockSpec((tm, MOBA_W), rows),
            pl.BlockSpec((tm, MEM_W), rows),
            resident(gmix), resident(gffn), resident(wg), resident(wsb), resident(wmo), resident(wme),
            resident(wout), resident(wfi), resident(wfd),
        ],
        out_specs=pl.BlockSpec((tm, d), rows),
        out_shape=jax.ShapeDtypeStruct((n, d), F32),
        compiler_params=pltpu.CompilerParams(dimension_semantics=("arbitrary",), vmem_limit_bytes=VMEM_LIMIT),
        name="merge_out_ffn",
    )(x2, osb, omo, ome, gmix, gffn, wg, wsb, wmo, wme, wout, wfi, wfd)


def _rope_tables(seq):
    half = HEAD_DIM // 2
    inv_freq = ROPE_THETA ** (-jnp.arange(half, dtype=F32) * 2.0 / HEAD_DIM)
    ang = jnp.arange(seq, dtype=F32)[:, None] * inv_freq[None, :]
    cos, sin = jnp.cos(ang), jnp.sin(ang)
    reps = LANES // HEAD_DIM
    return jnp.tile(jnp.concatenate([cos, cos], axis=1), (1, reps)), jnp.tile(jnp.concatenate([-sin, sin], axis=1), (1, reps))


def _head_mean_matrix(width):
    r = jnp.arange(width) // HEAD_DIM
    return jnp.where(r[:, None] == r[None, :], 1.0 / HEAD_DIM, 0.0).astype(BF16)


def _layer(x, mem, mix_norm_g, mem_norm_g, ffn_norm_g, w_in, w_mem_kv, moba_q_norm_g, moba_k_norm_g,
           mem_q_norm_g, mem_k_norm_g, w_up_sb, w_up_moba, w_up_mem, w_out, w_ffn_in, w_ffn_down):
    b, s, d = x.shape
    x2 = x.reshape(b * s, d)
    cos, sin = _rope_tables(s)
    bd = _head_mean_matrix(MXU_TILE)
    row = lambda g, reps=1: jnp.tile(g.astype(F32), reps)[None, :]
    gqk = jnp.concatenate([row(moba_q_norm_g, MOBA_HEADS), row(moba_k_norm_g, MOBA_HEADS)], axis=1)

    ck, cv = _memkv(mem, row(mem_norm_g), w_mem_kv.astype(BF16), bd, row(mem_k_norm_g, MEM_HEADS))
    sbq, sbk, sbv, moqt, mok, movt, meq, kmean = _inproj(
        x2, row(mix_norm_g), _split_projection(w_in[:, :QKV_W].astype(BF16)), cos, sin, bd, gqk,
        row(mem_q_norm_g, MEM_HEADS), s)
    seq3 = lambda t: t.reshape(b, s, t.shape[-1])
    o_sb = _sb_attention(seq3(sbq), seq3(sbk), seq3(sbv))
    o_mo = _moba_attention(moqt, seq3(mok), movt, kmean.reshape(b, s // MOBA_BLOCK, MOBA_W), gqk)
    o_me = _mem_attention(seq3(meq), ck, cv)
    flat = lambda t: t.reshape(b * s, t.shape[-1])
    out = _tail(x2, flat(o_sb), flat(o_mo), flat(o_me), row(mix_norm_g), row(ffn_norm_g),
                w_in[:, QKV_W:].astype(BF16), w_up_sb.astype(BF16), w_up_moba.astype(BF16), w_up_mem.astype(BF16),
                w_out.astype(BF16), w_ffn_in.astype(BF16), w_ffn_down.astype(BF16))
    return out.reshape(b, s, d)


def kernel(x, mem, mix_norm_g, mem_norm_g, ffn_norm_g, w_in, w_mem_kv, moba_q_norm_g, moba_k_norm_g, mem_q_norm_g, mem_k_norm_g, w_up_sb, w_up_moba, w_up_mem, w_out, w_ffn_in, w_ffn_down):
    for l in range(w_in.shape[0]):
        x = _layer(x, mem, mix_norm_g[l], mem_norm_g[l], ffn_norm_g[l], w_in[l], w_mem_kv[l],
                   moba_q_norm_g[l], moba_k_norm_g[l], mem_q_norm_g[l], mem_k_norm_g[l],
                   w_up_sb[l], w_up_moba[l], w_up_mem[l], w_out[l], w_ffn_in[l], w_ffn_down[l])
    return x
```

```python
import math

import jax
import jax.numpy as jnp
from jax import lax
from jax.experimental import pallas as pl
from jax.experimental.pallas import tpu as pltpu

F32 = jnp.float32
BF16 = jnp.bfloat16

HEAD_DIM = 64
SB_HEADS = 6
MOBA_HEADS = 6
MEM_HEADS = 4
SB_W = SB_HEADS * HEAD_DIM
MOBA_W = MOBA_HEADS * HEAD_DIM
MEM_W = MEM_HEADS * HEAD_DIM
QKV_W = 3 * SB_W + 3 * MOBA_W + MEM_W
MOBA_BLOCK = 256
MOBA_TOPK = 3
ROPE_THETA = 10000.0
EPS = 1e-6
QK_SCALE = 1.0 / math.sqrt(HEAD_DIM)

LANES = 128
MXU_TILE = 256
BF16_SUBLANES = 16
VT_ROWS = HEAD_DIM + BF16_SUBLANES
NEG_BIG = -1e30

SB_DONE_LOG = -30.0

INPROJ_ROWS = 1024
SB_ROWS = 1024
SB_SUB = 64
SB_WIN = 128
SB_LOOKAHEAD = 10
MOBA_LOOKAHEAD = 2
MOBA_QBLOCKS = 4
MOBA_LOOP_BLOCKS = 4
MOBA_ENTRY_BLOCKS = 2
MOBA_ENTRY_LOOKAHEAD = 3
MOBA_SAFE_LOGIT = 40.0
SCORE_BOUND_MARGIN = 1.1
MEMATTN_ROWS = 2048
MEMATTN_LOOKAHEAD = 5
TAIL_ROWS = 512
VMEM_LIMIT = 56 * 1024 * 1024


def _dot(a, b):
    return jnp.dot(a, b, preferred_element_type=F32)


def _dot_nt(a, b):
    return lax.dot_general(a, b, (((1,), (1,)), ((), ())), preferred_element_type=F32)


def _split_bf16(t):
    hi = t.astype(BF16)
    lo = (t - hi.astype(F32)).astype(BF16)
    return hi, lo


def _rms_rows(x, g):
    ms = jnp.mean(x * x, axis=-1, keepdims=True)
    return x * lax.rsqrt(ms + EPS) * g


def _head_rms(t, bd, g):
    hi, lo = _split_bf16(t * t)
    parts = []
    for c in range(0, t.shape[1], MXU_TILE):
        w = min(MXU_TILE, t.shape[1] - c)
        parts.append(_dot(hi[:, c:c + w], bd[:w, :w]) + _dot(lo[:, c:c + w], bd[:w, :w]))
    ms = parts[0] if len(parts) == 1 else jnp.concatenate(parts, axis=1)
    return t * lax.rsqrt(ms + EPS) * g


def _rotate_half_pairs(t):
    lane = lax.broadcasted_iota(jnp.int32, t.shape, 1)
    first_half = (lane & (HEAD_DIM - 1)) < (HEAD_DIM // 2)
    return jnp.where(first_half, pltpu.roll(t, LANES - HEAD_DIM // 2, 1), pltpu.roll(t, HEAD_DIM // 2, 1))


def _rope(t, cos, sin_signed):
    outs = []
    for c in range(t.shape[1] // LANES):
        tc = t[:, c * LANES:(c + 1) * LANES]
        outs.append(tc * cos + _rotate_half_pairs(tc) * sin_signed)
    return jnp.concatenate(outs, axis=1)


def _memkv_kernel(mem_ref, g_ref, w_ref, bd_ref, gk_ref, ck_ref, cvt_ref):
    h = _rms_rows(mem_ref[0], g_ref[...]).astype(BF16)
    kv = _dot(h, w_ref[...])
    k = kv[:, :MEM_W]
    ck_ref[0] = _head_rms(k, bd_ref[...], gk_ref[...]).astype(BF16)
    vt = kv[:, MEM_W:].T
    for hd in range(MEM_HEADS):
        cvt_ref[0, hd, 0:HEAD_DIM, :] = vt[hd * HEAD_DIM:(hd + 1) * HEAD_DIM, :].astype(BF16)
        cvt_ref[0, hd, HEAD_DIM:, :] = jnp.ones((VT_ROWS - HEAD_DIM, vt.shape[1]), BF16)


def _memkv(mem, g, w, bd, gk):
    b, m, d = mem.shape
    return pl.pallas_call(
        _memkv_kernel,
        grid=(b,),
        in_specs=[
            pl.BlockSpec((1, m, d), lambda i: (i, 0, 0)),
            pl.BlockSpec((1, d), lambda i: (0, 0)),
            pl.BlockSpec((d, 2 * MEM_W), lambda i: (0, 0)),
            pl.BlockSpec((MXU_TILE, MXU_TILE), lambda i: (0, 0)),
            pl.BlockSpec((1, MEM_W), lambda i: (0, 0)),
        ],
        out_specs=[
            pl.BlockSpec((1, m, MEM_W), lambda i: (i, 0, 0)),
            pl.BlockSpec((1, MEM_HEADS, VT_ROWS, m), lambda i: (i, 0, 0, 0)),
        ],
        out_shape=[jax.ShapeDtypeStruct((b, m, MEM_W), BF16),
                   jax.ShapeDtypeStruct((b, MEM_HEADS, VT_ROWS, m), BF16)],
        compiler_params=pltpu.CompilerParams(dimension_semantics=("arbitrary",), vmem_limit_bytes=VMEM_LIMIT),
        name="memkv",
    )(mem, g, w, bd, gk)


def _inproj_kernel(x_ref, g_ref, wa_ref, wb_ref, wc_ref, cos_ref, sin_ref, bd_ref, gqk_ref, gmq_ref,
                   sbq_ref, sbk_ref, sbv_ref, moqt_ref, mok_ref, movt_ref, meq_ref, kmean_ref):
    h = _rms_rows(x_ref[...], g_ref[...]).astype(BF16)
    pa = _dot(h, wa_ref[...])
    pb = _dot(h, wb_ref[...])
    bd = bd_ref[...]
    qk_normed = _head_rms(pa[:, :2 * MOBA_W], bd, gqk_ref[...])
    cq = _head_rms(pa[:, 3 * MOBA_W:3 * MOBA_W + MEM_W], bd, gmq_ref[...])
    pc = _dot(h, wc_ref[...])

    qk = _rope(qk_normed, cos_ref[...], sin_ref[...])
    moqt_ref[0] = (qk[:, :MOBA_W] * QK_SCALE).T.astype(BF16)
    mk = qk[:, MOBA_W:]
    mok_ref[...] = mk.astype(BF16)
    for blk in range(INPROJ_ROWS // MOBA_BLOCK):
        kmean_ref[0, blk:blk + 1, :] = jnp.mean(mk[blk * MOBA_BLOCK:(blk + 1) * MOBA_BLOCK], axis=0, keepdims=True)
    vt = pa[:, 2 * MOBA_W:3 * MOBA_W].T
    for hd in range(MOBA_HEADS):
        movt_ref[0, hd, 0:HEAD_DIM, :] = vt[hd * HEAD_DIM:(hd + 1) * HEAD_DIM, :].astype(BF16)
        movt_ref[0, hd, HEAD_DIM:, :] = jnp.ones((VT_ROWS - HEAD_DIM, vt.shape[1]), BF16)
    meq_ref[...] = (cq * QK_SCALE).astype(BF16)

    sb = jnp.concatenate([pa[:, 3 * MOBA_W + MEM_W:], pb, pc], axis=1)
    sbq_ref[...] = (sb[:, :SB_W] * QK_SCALE).astype(BF16)
    sbk_ref[...] = sb[:, SB_W:2 * SB_W].astype(BF16)
    sbv_ref[...] = sb[:, 2 * SB_W:].astype(BF16)


def _split_projection(w):
    sb = w[:, :3 * SB_W]
    rest = w[:, 3 * SB_W:]
    head = -rest.shape[1] % MXU_TILE
    mid = head + (sb.shape[1] - head) // (2 * MXU_TILE) * MXU_TILE
    return jnp.concatenate([rest, sb[:, :head]], axis=1), sb[:, head:mid], sb[:, mid:]


def _inproj(x2, g, ws, cos, sin, bd, gqk, gmq, seq):
    n, d = x2.shape
    tm = INPROJ_ROWS
    steps = n // tm
    per_seq = seq // tm
    blocks_per_step = tm // MOBA_BLOCK
    const = lambda i: (0, 0)
    rows = lambda i: (i, 0)
    out_shapes = [jax.ShapeDtypeStruct((n, SB_W), BF16)] * 3 + [
        jax.ShapeDtypeStruct((n // seq, MOBA_W, seq), BF16),
        jax.ShapeDtypeStruct((n, MOBA_W), BF16),
        jax.ShapeDtypeStruct((n // seq, MOBA_HEADS, VT_ROWS, seq), BF16),
        jax.ShapeDtypeStruct((n, MEM_W), BF16),
        jax.ShapeDtypeStruct((steps, blocks_per_step, MOBA_W), F32),
    ]
    out_specs = [pl.BlockSpec((tm, SB_W), rows)] * 3 + [
        pl.BlockSpec((1, MOBA_W, tm), lambda i: (i // per_seq, 0, i % per_seq)),
        pl.BlockSpec((tm, MOBA_W), rows),
        pl.BlockSpec((1, MOBA_HEADS, VT_ROWS, tm), lambda i: (i // per_seq, 0, 0, i % per_seq)),
        pl.BlockSpec((tm, MEM_W), rows),
        pl.BlockSpec((1, blocks_per_step, MOBA_W), lambda i: (i, 0, 0)),
    ]
    return pl.pallas_call(
        _inproj_kernel,
        grid=(steps,),
        in_specs=[
            pl.BlockSpec((tm, d), rows),
            pl.BlockSpec((1, d), const),
            *[pl.BlockSpec(w.shape, const) for w in ws],
            pl.BlockSpec((tm, LANES), lambda i: (i % per_seq, 0)),
            pl.BlockSpec((tm, LANES), lambda i: (i % per_seq, 0)),
            pl.BlockSpec((MXU_TILE, MXU_TILE), const),
            pl.BlockSpec((1, 2 * MOBA_W), const),
            pl.BlockSpec((1, MEM_W), const),
        ],
        out_specs=out_specs,
        out_shape=out_shapes,
        compiler_params=pltpu.CompilerParams(dimension_semantics=("arbitrary",), vmem_limit_bytes=VMEM_LIMIT),
        name="inproj",
    )(x2, g, *ws, cos, sin, bd, gqk, gmq)


def _sb_kernel(q_ref, k_ref, v_ref, o_ref):
    i = pl.program_id(1)
    lane = lax.broadcasted_iota(jnp.int32, (SB_SUB, LANES), 1)
    low = lane < HEAD_DIM
    kiota = lax.broadcasted_iota(jnp.int32, (SB_SUB, SB_WIN), 1)
    riota = lax.broadcasted_iota(jnp.int32, (SB_SUB, 1), 0)
    ur = lax.broadcasted_iota(jnp.int32, (SB_WIN, SB_WIN), 0)
    uc = lax.broadcasted_iota(jnp.int32, (SB_WIN, SB_WIN), 1)
    later = (ur > uc).astype(BF16)
    n_pairs = SB_W // LANES

    n_sub = SB_ROWS // SB_SUB
    chains = []
    for sub in range(n_sub):
        for pair in range(n_pairs):
            q2 = q_ref[0, sub * SB_SUB:(sub + 1) * SB_SUB, pair * LANES:(pair + 1) * LANES]
            zero = jnp.zeros_like(q2)
            chains += [(sub, pair, jnp.where(low, q2, zero)), (sub, pair, jnp.where(low, zero, q2))]
    n_chains = len(chains)

    def scan_windows(starts, limits, carries):
        def cols(c):
            return slice(chains[c][1] * LANES, (chains[c][1] + 1) * LANES)

        def scores(c):
            sub = chains[c][0]
            return _dot_nt(chains[c][2], k_ref[0, pl.ds(starts[sub], SB_WIN), cols(c)])

        valids = [(starts[sub] + kiota) < limits[sub] for sub in range(n_sub)]

        def scan(c, z):
            valid = valids[chains[c][0]]
            log_beta = jnp.minimum(z, 0.0) - jnp.log(1.0 + jnp.exp(-jnp.abs(z)))
            log_1m = jnp.where(valid, log_beta - z, 0.0)
            hi, lo = _split_bf16(log_1m)
            tail = _dot(hi, later) + _dot(lo, later)
            return log_beta, tail, carries[c] + jnp.sum(log_1m, axis=1, keepdims=True)

        def weigh(c, log_beta, tail):
            sub = chains[c][0]
            a = jnp.where(valids[sub], jnp.exp(log_beta + tail + carries[c]), 0.0)
            return _dot(a.astype(BF16), v_ref[0, pl.ds(starts[sub], SB_WIN), cols(c)])

        zs, scans, outs = {}, {}, [None] * n_chains
        for t in range(n_chains + 2 * SB_LOOKAHEAD):
            if t < n_chains:
                zs[t] = scores(t)
            c = t - SB_LOOKAHEAD
            if 0 <= c < n_chains:
                scans[c] = scan(c, zs.pop(c))
            c = t - 2 * SB_LOOKAHEAD
            if 0 <= c < n_chains:
                log_beta, tail, carry = scans.pop(c)
                outs[c] = (weigh(c, log_beta, tail), carry)
        return outs

    def unfinished(cs, starts):
        worst = None
        for c, carry in enumerate(cs):
            open_carry = jnp.where(starts[chains[c][0]] > 0, carry, NEG_BIG)
            worst = open_carry if worst is None else jnp.maximum(worst, open_carry)
        return (jnp.max(worst) > SB_DONE_LOG).astype(jnp.int32)

    t0s = [i * SB_ROWS + sub * SB_SUB for sub in range(n_sub)]
    zero_carries = [jnp.zeros((SB_SUB, 1), F32)] * n_chains

    def window_before(ends):
        starts = [pl.multiple_of(jnp.maximum(e - SB_WIN, 0), SB_SUB) for e in ends]
        limits = [jnp.minimum(t0 + riota, e) for t0, e in zip(t0s, ends)]
        return starts, limits

    def write(accs):
        for sub in range(n_sub):
            for pair in range(n_pairs):
                c = (sub * n_pairs + pair) * 2
                out = jnp.where(low, accs[c], accs[c + 1])
                o_ref[0, sub * SB_SUB:(sub + 1) * SB_SUB, pair * LANES:(pair + 1) * LANES] = out.astype(BF16)

    ends0 = [t0 + SB_SUB for t0 in t0s]
    starts0, limits0 = window_before(ends0)
    first = scan_windows(starts0, limits0, zero_carries)
    write([acc for acc, _ in first])
    @pl.when(unfinished([carry for _, carry in first], starts0) > 0)
    def _():
        def cond(state):
            ends, more = state[:n_sub], state[n_sub]
            return jnp.logical_and(ends[-1] > 0, more > 0)

        def body(state):
            ends = state[:n_sub]
            accs_c = state[n_sub + 1:n_sub + 1 + n_chains]
            cs_c = state[n_sub + 1 + n_chains:]
            starts, limits = window_before(ends)
            res = scan_windows(starts, limits, list(cs_c))
            new_cs = [r[1] for r in res]
            return (*starts, unfinished(new_cs, starts), *[a + r[0] for a, r in zip(accs_c, res)], *new_cs)

        zero_accs = [jnp.zeros((SB_SUB, LANES), F32)] * n_chains
        state = lax.while_loop(cond, body, (*ends0, jnp.int32(1), *zero_accs, *zero_carries))
        write(state[n_sub + 1:n_sub + 1 + n_chains])


def _sb_attention(q, k, v):
    b, s, w = q.shape
    return pl.pallas_call(
        _sb_kernel,
        grid=(b, s // SB_ROWS),
        in_specs=[
            pl.BlockSpec((1, SB_ROWS, w), lambda bi, i: (bi, i, 0)),
            pl.BlockSpec((1, s, w), lambda bi, i: (bi, 0, 0)),
            pl.BlockSpec((1, s, w), lambda bi, i: (bi, 0, 0)),
        ],
        out_specs=pl.BlockSpec((1, SB_ROWS, w), lambda bi, i: (bi, i, 0)),
        out_shape=jax.ShapeDtypeStruct((b, s, w), BF16),
        compiler_params=pltpu.CompilerParams(dimension_semantics=("arbitrary", "arbitrary"),
                                             vmem_limit_bytes=VMEM_LIMIT),
        name="sb_attention",
    )(q, k, v)


def _moba_kernel(qt_ref, k_ref, vt_ref, km_ref, gqk_ref, o_ref):
    first = pl.program_id(1) * MOBA_QBLOCKS
    nb = km_ref.shape[1]
    tq = MOBA_BLOCK
    n_heads = vt_ref.shape[1]
    blk = lax.broadcasted_iota(jnp.int32, (nb, tq), 0)
    items = [(qb, h) for qb in range(MOBA_QBLOCKS) for h in range(n_heads)]

    def cols(h):
        pair = h // 2
        return slice(pair * LANES, (pair + 1) * LANES)

    first_head_rows = lax.broadcasted_iota(jnp.int32, (LANES, MOBA_QBLOCKS * tq), 0) < HEAD_DIM
    qt_heads = []
    for pair in range(n_heads // 2):
        q2 = qt_ref[0, pair * LANES:(pair + 1) * LANES, :]
        zero = jnp.zeros_like(q2)
        qt_heads += [jnp.where(first_head_rows, q2, zero), jnp.where(first_head_rows, zero, q2)]
    qts = [qt_heads[h][:, qb * tq:(qb + 1) * tq] for qb, h in items]

    bits = [None] * len(items)

    def select_blocks():
        km_split = [_split_bf16(km_ref[0, :, cols(h)]) for h in range(n_heads)]
        gates = [_dot(km_split[h][0], qts[it]) + _dot(km_split[h][1], qts[it])
                 for it, (_, h) in enumerate(items)]
        blkf = blk.astype(F32)
        for it, (qb, _) in enumerate(items):
            gate = jnp.where(blk < first + qb, gates[it], -jnp.inf)
            chosen = jnp.zeros((1, tq), jnp.int32)
            for _ in range(MOBA_TOPK):
                best = jnp.max(gate, axis=0, keepdims=True)
                idx = jnp.min(jnp.where(gate == best, blkf, float(nb)), axis=0, keepdims=True)
                chosen = chosen | jnp.where(best > -jnp.inf, jnp.left_shift(1, idx.astype(jnp.int32)), 0)
                gate = jnp.where(blkf == idx, -jnp.inf, gate)
            bits[it] = chosen

    def block_offset(j):
        return pl.multiple_of(j * tq, tq)

    def pipelined(work, scores, reduce, lookahead=MOBA_LOOKAHEAD):
        queued = {n: scores(w) for n, w in enumerate(work[:lookahead])}
        for n, w in enumerate(work):
            if n + lookahead < len(work):
                queued[n + lookahead] = scores(work[n + lookahead])
            reduce(w, queued.pop(n))

    causal = lax.broadcasted_iota(jnp.int32, (tq, tq), 0) <= lax.broadcasted_iota(jnp.int32, (tq, tq), 1)
    n_items = len(items)
    all_items = list(range(n_items))

    def own_scores(it):
        qb, h = items[it]
        return _dot(k_ref[0, pl.ds(block_offset(first + qb), tq), cols(h)], qts[it])

    def past_scores(blocks_of):
        def scores(it):
            h = items[it][1]
            return [_dot(k_ref[0, pl.ds(block_offset(j), tq), cols(h)], qts[it]) for j in blocks_of(it)]
        return scores

    def v_rows(it, j):
        return vt_ref[0, items[it][1], :, pl.ds(block_offset(j), tq)]

    def picked(it, j):
        return (jnp.right_shift(bits[it], j) & 1) > 0

    def earlier_in_step(it):
        return [first + e for e in range(items[it][0])]

    def loop_blocks(jj):
        return lambda it: [MOBA_LOOP_BLOCKS * jj + u for u in range(MOBA_LOOP_BLOCKS)]

    loop_trips = first // MOBA_LOOP_BLOCKS

    def with_earlier(qb):
        return [it for it in all_items if items[it][0] == qb]

    def stable_path():
        state = [None] * (2 * n_items)
        select_blocks()

        def own_reduce(it, raw):
            s = jnp.where(causal, raw, NEG_BIG)
            m = jnp.max(s, axis=0, keepdims=True)
            state[2 * it] = m
            state[2 * it + 1] = _dot(v_rows(it, first + items[it][0]), jnp.exp(s - m).astype(BF16))

        pipelined(all_items, own_scores, own_reduce)

        def attend(st, work, blocks_of):
            def reduce(it, raws):
                m, acc = st[2 * it], st[2 * it + 1]
                picks = [picked(it, j) for j in blocks_of(it)]
                m_new = m
                for pick, s in zip(picks, raws):
                    m_new = jnp.maximum(m_new, jnp.where(pick, jnp.max(s, axis=0, keepdims=True), NEG_BIG))
                acc = jnp.exp(m - m_new) * acc
                for j, pick, s in zip(blocks_of(it), picks, raws):
                    p = jnp.exp((s - m_new).astype(BF16))
                    acc = acc + _dot(v_rows(it, j), jnp.where(pick, p, jnp.zeros_like(p)))
                st[2 * it] = m_new
                st[2 * it + 1] = acc
            pipelined(work, past_scores(blocks_of), reduce)

        for qb in range(1, MOBA_QBLOCKS):
            attend(state, with_earlier(qb), earlier_in_step)

        def body(jj, st):
            st = list(st)
            attend(st, all_items, loop_blocks(jj))
            return tuple(st)

        state = lax.fori_loop(0, loop_trips, body, tuple(state))
        return tuple(state[1::2])

    def bounded_path():
        def attend(st, blocks_of):
            work = []
            for it in all_items:
                blocks = blocks_of(it)
                work += [(it, blocks[c:c + MOBA_ENTRY_BLOCKS]) for c in range(0, len(blocks), MOBA_ENTRY_BLOCKS)]

            def scores(entry):
                it, blocks = entry
                return [_dot(k_ref[0, pl.ds(block_offset(j), tq), cols(items[it][1])], qts[it]) for j, _ in blocks]

            def reduce(entry, raws):
                it, blocks = entry
                for (j, is_own), s in zip(blocks, raws):
                    p = jnp.exp(s).astype(BF16)
                    p = jnp.where(causal if is_own else picked(it, j), p, jnp.zeros_like(p))
                    term = _dot(v_rows(it, j), p)
                    st[it] = term if st[it] is None else st[it] + term

            pipelined(work, scores, reduce, MOBA_ENTRY_LOOKAHEAD)

        select_blocks()
        accs = [None] * n_items
        attend(accs, lambda it: [(j, False) for j in earlier_in_step(it)] + [(first + items[it][0], True)])

        def body(jj, st):
            st = list(st)
            attend(st, lambda it: [(j, False) for j in loop_blocks(jj)(it)])
            return tuple(st)

        return lax.fori_loop(0, loop_trips, body, tuple(accs))

    gq = gqk_ref[:, :MOBA_W]
    gk = gqk_ref[:, MOBA_W:]
    score_bound_sq = (HEAD_DIM * SCORE_BOUND_MARGIN) * jnp.max(gq * gq) * jnp.max(gk * gk)
    accs = lax.cond(score_bound_sq <= MOBA_SAFE_LOGIT * MOBA_SAFE_LOGIT, bounded_path, stable_path)
    for qb in range(MOBA_QBLOCKS):
        for pair in range(n_heads // 2):
            outs = []
            for h in (2 * pair, 2 * pair + 1):
                acc = accs[qb * n_heads + h]
                outs.append(acc[:HEAD_DIM] / acc[HEAD_DIM:HEAD_DIM + 1])
            o_ref[0, qb * tq:(qb + 1) * tq, pair * LANES:(pair + 1) * LANES] = (
                jnp.concatenate(outs, axis=0).T.astype(BF16))


def _moba_attention(qt, k, vt, kmean, gqk):
    b, s, w = k.shape
    nb = s // MOBA_BLOCK
    rows = MOBA_QBLOCKS * MOBA_BLOCK
    return pl.pallas_call(
        _moba_kernel,
        grid=(b, nb // MOBA_QBLOCKS),
        in_specs=[
            pl.BlockSpec((1, w, rows), lambda bi, i: (bi, 0, i)),
            pl.BlockSpec((1, s, w), lambda bi, i: (bi, 0, 0)),
            pl.BlockSpec((1, w // HEAD_DIM, VT_ROWS, s), lambda bi, i: (bi, 0, 0, 0)),
            pl.BlockSpec((1, nb, w), lambda bi, i: (bi, 0, 0)),
            pl.BlockSpec((1, 2 * w), lambda bi, i: (0, 0)),
        ],
        out_specs=pl.BlockSpec((1, rows, w), lambda bi, i: (bi, i, 0)),
        out_shape=jax.ShapeDtypeStruct((b, s, w), BF16),
        compiler_params=pltpu.CompilerParams(dimension_semantics=("arbitrary",) * 2, vmem_limit_bytes=VMEM_LIMIT),
        name="moba_attention",
    )(qt, k, vt, kmean, gqk)


def _memattn_kernel(q_ref, ck_ref, cvt_ref, o_ref):
    tq = MXU_TILE
    lane = lax.broadcasted_iota(jnp.int32, (tq, LANES), 1)
    low = lane < HEAD_DIM
    work = [(tile, hd) for tile in range(q_ref.shape[1] // tq) for hd in range(MEM_HEADS)]

    def cols(hd):
        return slice((hd // 2) * LANES, (hd // 2 + 1) * LANES)

    def scores(w):
        tile, hd = work[w]
        q2 = q_ref[0, tile * tq:(tile + 1) * tq, cols(hd)]
        zero = jnp.zeros_like(q2)
        qm = jnp.where(low, q2, zero) if hd % 2 == 0 else jnp.where(low, zero, q2)
        return _dot_nt(ck_ref[0, :, cols(hd)], qm)

    queued = {w: scores(w) for w in range(MEMATTN_LOOKAHEAD)}
    outs = {}
    for w, (tile, hd) in enumerate(work):
        if w + MEMATTN_LOOKAHEAD < len(work):
            queued[w + MEMATTN_LOOKAHEAD] = scores(w + MEMATTN_LOOKAHEAD)
        s = queued.pop(w)
        p = jnp.exp(s - jnp.max(s, axis=0, keepdims=True)).astype(BF16)
        acc = _dot(cvt_ref[0, hd], p)
        outs[hd % 2] = acc[:HEAD_DIM] / acc[HEAD_DIM:HEAD_DIM + 1]
        if hd % 2 == 1:
            o_ref[0, tile * tq:(tile + 1) * tq, cols(hd)] = (
                jnp.concatenate([outs[0], outs[1]], axis=0).T.astype(BF16))


def _mem_attention(q, ck, cv):
    b, s, w = q.shape
    m = ck.shape[1]
    tm = MEMATTN_ROWS
    return pl.pallas_call(
        _memattn_kernel,
        grid=(b, s // tm),
        in_specs=[
            pl.BlockSpec((1, tm, w), lambda bi, i: (bi, i, 0)),
            pl.BlockSpec((1, m, w), lambda bi, i: (bi, 0, 0)),
            pl.BlockSpec((1, MEM_HEADS, VT_ROWS, m), lambda bi, i: (bi, 0, 0, 0)),
        ],
        out_specs=pl.BlockSpec((1, tm, w), lambda bi, i: (bi, i, 0)),
        out_shape=jax.ShapeDtypeStruct((b, s, w), BF16),
        compiler_params=pltpu.CompilerParams(dimension_semantics=("arbitrary", "arbitrary"),
                                             vmem_limit_bytes=VMEM_LIMIT),
        name="mem_attention",
    )(q, ck, cv)


def _sigmoid(t):
    return 1.0 / (1.0 + jnp.exp(-t))


def _tail_kernel(x_ref, osb_ref, omo_ref, ome_ref, gmix_ref, gffn_ref, wg_ref, wsb_ref, wmo_ref, wme_ref,
                 wout_ref, wfi_ref, wfd_ref, out_ref):
    d = x_ref.shape[1]
    dff = wfd_ref.shape[0]
    half = x_ref.shape[0] // 2

    def merged(rows):
        x = x_ref[rows, :]
        h = _rms_rows(x, gmix_ref[...]).astype(BF16)
        mix = None
        for n, (o_ref, wu_ref) in enumerate(((osb_ref, wsb_ref), (omo_ref, wmo_ref), (ome_ref, wme_ref))):
            gate = _dot(h, wg_ref[:, n * d:(n + 1) * d])
            term = _sigmoid(gate) * _dot(o_ref[rows, :], wu_ref[...])
            mix = term if mix is None else mix + term
        return x + _dot(mix.astype(BF16), wout_ref[...])

    def swiglu(x1):
        h2 = _rms_rows(x1, gffn_ref[...]).astype(BF16)
        gate = _dot(h2, wfi_ref[:, :dff])
        up = _dot(h2, wfi_ref[:, dff:])
        ff = (gate * _sigmoid(gate) * up).astype(BF16)
        return x1 + _dot(ff, wfd_ref[...])

    halves = [slice(r * half, (r + 1) * half) for r in range(2)]
    x1s = [merged(rows) for rows in halves]
    for rows, x1 in zip(halves, x1s):
        out_ref[rows, :] = swiglu(x1)


def _tail(x2, osb, omo, ome, gmix, gffn, wg, wsb, wmo, wme, wout, wfi, wfd):
    n, d = x2.shape
    tm = TAIL_ROWS
    rows = lambda i: (i, 0)

    def resident(arr):
        return pl.BlockSpec(arr.shape, lambda i: (0, 0), pipeline_mode=pl.Buffered(1))

    return pl.pallas_call(
        _tail_kernel,
        grid=(n // tm,),
        in_specs=[
            pl.BlockSpec((tm, d), rows),
            pl.BlockSpec((tm, SB_W), rows),
            pl.BlockSpec((tm, MOBA_W), rows),
            pl.BlockSpec((tm, MEM_W), rows),
            resident(gmix), resident(gffn), resident(wg), resident(wsb), resident(wmo), resident(wme),
            resident(wout), resident(wfi), resident(wfd),
        ],
        out_specs=pl.BlockSpec((tm, d), rows),
        out_shape=jax.ShapeDtypeStruct((n, d), F32),
        compiler_params=pltpu.CompilerParams(dimension_semantics=("arbitrary",), vmem_limit_bytes=VMEM_LIMIT),
        name="merge_out_ffn",
    )(x2, osb, omo, ome, gmix, gffn, wg, wsb, wmo, wme, wout, wfi, wfd)


def _rope_tables(seq):
    half = HEAD_DIM // 2
    inv_freq = ROPE_THETA ** (-jnp.arange(half, dtype=F32) * 2.0 / HEAD_DIM)
    ang = jnp.arange(seq, dtype=F32)[:, None] * inv_freq[None, :]
    cos, sin = jnp.cos(ang), jnp.sin(ang)
    reps = LANES // HEAD_DIM
    return jnp.tile(jnp.concatenate([cos, cos], axis=1), (1, reps)), jnp.tile(jnp.concatenate([-sin, sin], axis=1), (1, reps))


def _head_mean_matrix(width):
    r = jnp.arange(width) // HEAD_DIM
    return jnp.where(r[:, None] == r[None, :], 1.0 / HEAD_DIM, 0.0).astype(BF16)


def _layer(x, mem, mix_norm_g, mem_norm_g, ffn_norm_g, w_in, w_mem_kv, moba_q_norm_g, moba_k_norm_g,
           mem_q_norm_g, mem_k_norm_g, w_up_sb, w_up_moba, w_up_mem, w_out, w_ffn_in, w_ffn_down):
    b, s, d = x.shape
    x2 = x.reshape(b * s, d)
    cos, sin = _rope_tables(s)
    bd = _head_mean_matrix(MXU_TILE)
    row = lambda g, reps=1: jnp.tile(g.astype(F32), reps)[None, :]
    gqk = jnp.concatenate([row(moba_q_norm_g, MOBA_HEADS), row(moba_k_norm_g, MOBA_HEADS)], axis=1)

    ck, cv = _memkv(mem, row(mem_norm_g), w_mem_kv.astype(BF16), bd, row(mem_k_norm_g, MEM_HEADS))
    sbq, sbk, sbv, moqt, mok, movt, meq, kmean = _inproj(
        x2, row(mix_norm_g), _split_projection(w_in[:, :QKV_W].astype(BF16)), cos, sin, bd, gqk,
        row(mem_q_norm_g, MEM_HEADS), s)
    seq3 = lambda t: t.reshape(b, s, t.shape[-1])
    o_sb = _sb_attention(seq3(sbq), seq3(sbk), seq3(sbv))
    o_mo = _moba_attention(moqt, seq3(mok), movt, kmean.reshape(b, s // MOBA_BLOCK, MOBA_W), gqk)
    o_me = _mem_attention(seq3(meq), ck, cv)
    flat = lambda t: t.reshape(b * s, t.shape[-1])
    out = _tail(x2, flat(o_sb), flat(o_mo), flat(o_me), row(mix_norm_g), row(ffn_norm_g),
                w_in[:, QKV_W:].astype(BF16), w_up_sb.astype(BF16), w_up_moba.astype(BF16), w_up_mem.astype(BF16),
                w_out.astype(BF16), w_ffn_in.astype(BF16), w_ffn_down.astype(BF16))
    return out.reshape(b, s, d)


def kernel(x, mem, mix_norm_g, mem_norm_g, ffn_norm_g, w_in, w_mem_kv, moba_q_norm_g, moba_k_norm_g, mem_q_norm_g, mem_k_norm_g, w_up_sb, w_up_moba, w_up_mem, w_out, w_ffn_in, w_ffn_down):
    for l in range(w_in.shape[0]):
        x = _layer(x, mem, mix_norm_g[l], mem_norm_g[l], ffn_norm_g[l], w_in[l], w_mem_kv[l],
                   moba_q_norm_g[l], moba_k_norm_g[l], mem_q_norm_g[l], mem_k_norm_g[l],
                   w_up_sb[l], w_up_moba[l], w_up_mem[l], w_out[l], w_ffn_in[l], w_ffn_down[l])
    return x
```

```python
import math

import jax
import jax.numpy as jnp
from jax import lax
from jax.experimental import pallas as pl
from jax.experimental.pallas import tpu as pltpu

F32 = jnp.float32
BF16 = jnp.bfloat16

HEAD_DIM = 64
SB_HEADS = 6
MOBA_HEADS = 6
MEM_HEADS = 4
SB_W = SB_HEADS * HEAD_DIM
MOBA_W = MOBA_HEADS * HEAD_DIM
MEM_W = MEM_HEADS * HEAD_DIM
QKV_W = 3 * SB_W + 3 * MOBA_W + MEM_W
MOBA_BLOCK = 256
MOBA_TOPK = 3
ROPE_THETA = 10000.0
EPS = 1e-6
QK_SCALE = 1.0 / math.sqrt(HEAD_DIM)

LANES = 128
MXU_TILE = 256
BF16_SUBLANES = 16
VT_ROWS = HEAD_DIM + BF16_SUBLANES
NEG_BIG = -1e30

SB_DONE_LOG = -30.0

INPROJ_ROWS = 1024
SB_ROWS = 1024
SB_SUB = 64
SB_WIN = 128
SB_LOOKAHEAD = 10
MOBA_LOOKAHEAD = 2
MOBA_QBLOCKS = 4
MOBA_LOOP_BLOCKS = 4
MOBA_ENTRY_BLOCKS = 2
MOBA_ENTRY_LOOKAHEAD = 3
MOBA_SAFE_LOGIT = 40.0
SCORE_BOUND_MARGIN = 1.1
MEMATTN_ROWS = 2048
MEMATTN_LOOKAHEAD = 5
TAIL_ROWS = 512
VMEM_LIMIT = 56 * 1024 * 1024


def _dot(a, b):
    return jnp.dot(a, b, preferred_element_type=F32)


def _dot_nt(a, b):
    return lax.dot_general(a, b, (((1,), (1,)), ((), ())), preferred_element_type=F32)


def _split_bf16(t):
    hi = t.astype(BF16)
    lo = (t - hi.astype(F32)).astype(BF16)
    return hi, lo


def _rms_rows(x, g):
    ms = jnp.mean(x * x, axis=-1, keepdims=True)
    return x * lax.rsqrt(ms + EPS) * g


def _head_rms(t, bd, g):
    hi, lo = _split_bf16(t * t)
    parts = []
    for c in range(0, t.shape[1], MXU_TILE):
        w = min(MXU_TILE, t.shape[1] - c)
        parts.append(_dot(hi[:, c:c + w], bd[:w, :w]) + _dot(lo[:, c:c + w], bd[:w, :w]))
    ms = parts[0] if len(parts) == 1 else jnp.concatenate(parts, axis=1)
    return t * lax.rsqrt(ms + EPS) * g


def _rotate_half_pairs(t):
    lane = lax.broadcasted_iota(jnp.int32, t.shape, 1)
    first_half = (lane & (HEAD_DIM - 1)) < (HEAD_DIM // 2)
    return jnp.where(first_half, pltpu.roll(t, LANES - HEAD_DIM // 2, 1), pltpu.roll(t, HEAD_DIM // 2, 1))


def _rope(t, cos, sin_signed):
    outs = []
    for c in range(t.shape[1] // LANES):
        tc = t[:, c * LANES:(c + 1) * LANES]
        outs.append(tc * cos + _rotate_half_pairs(tc) * sin_signed)
    return jnp.concatenate(outs, axis=1)


def _memkv_kernel(mem_ref, g_ref, w_ref, bd_ref, gk_ref, ck_ref, cvt_ref):
    h = _rms_rows(mem_ref[0], g_ref[...]).astype(BF16)
    kv = _dot(h, w_ref[...])
    k = kv[:, :MEM_W]
    ck_ref[0] = _head_rms(k, bd_ref[...], gk_ref[...]).astype(BF16)
    vt = kv[:, MEM_W:].T
    for hd in range(MEM_HEADS):
        cvt_ref[0, hd, 0:HEAD_DIM, :] = vt[hd * HEAD_DIM:(hd + 1) * HEAD_DIM, :].astype(BF16)
        cvt_ref[0, hd, HEAD_DIM:, :] = jnp.ones((VT_ROWS - HEAD_DIM, vt.shape[1]), BF16)


def _memkv(mem, g, w, bd, gk):
    b, m, d = mem.shape
    return pl.pallas_call(
        _memkv_kernel,
        grid=(b,),
        in_specs=[
            pl.BlockSpec((1, m, d), lambda i: (i, 0, 0)),
            pl.BlockSpec((1, d), lambda i: (0, 0)),
            pl.BlockSpec((d, 2 * MEM_W), lambda i: (0, 0)),
            pl.BlockSpec((MXU_TILE, MXU_TILE), lambda i: (0, 0)),
            pl.BlockSpec((1, MEM_W), lambda i: (0, 0)),
        ],
        out_specs=[
            pl.BlockSpec((1, m, MEM_W), lambda i: (i, 0, 0)),
            pl.BlockSpec((1, MEM_HEADS, VT_ROWS, m), lambda i: (i, 0, 0, 0)),
        ],
        out_shape=[jax.ShapeDtypeStruct((b, m, MEM_W), BF16),
                   jax.ShapeDtypeStruct((b, MEM_HEADS, VT_ROWS, m), BF16)],
        compiler_params=pltpu.CompilerParams(dimension_semantics=("arbitrary",), vmem_limit_bytes=VMEM_LIMIT),
        name="memkv",
    )(mem, g, w, bd, gk)


def _inproj_kernel(x_ref, g_ref, wa_ref, wb_ref, wc_ref, cos_ref, sin_ref, bd_ref, gqk_ref, gmq_ref,
                   sbq_ref, sbk_ref, sbv_ref, moqt_ref, mok_ref, movt_ref, meq_ref, kmean_ref):
    h = _rms_rows(x_ref[...], g_ref[...]).astype(BF16)
    pa = _dot(h, wa_ref[...])
    pb = _dot(h, wb_ref[...])
    bd = bd_ref[...]
    qk_normed = _head_rms(pa[:, :2 * MOBA_W], bd, gqk_ref[...])
    cq = _head_rms(pa[:, 3 * MOBA_W:3 * MOBA_W + MEM_W], bd, gmq_ref[...])
    pc = _dot(h, wc_ref[...])

    qk = _rope(qk_normed, cos_ref[...], sin_ref[...])
    moqt_ref[0] = (qk[:, :MOBA_W] * QK_SCALE).T.astype(BF16)
    mk = qk[:, MOBA_W:]
    mok_ref[...] = mk.astype(BF16)
    for blk in range(INPROJ_ROWS // MOBA_BLOCK):
        kmean_ref[0, blk:blk + 1, :] = jnp.mean(mk[blk * MOBA_BLOCK:(blk + 1) * MOBA_BLOCK], axis=0, keepdims=True)
    vt = pa[:, 2 * MOBA_W:3 * MOBA_W].T
    for hd in range(MOBA_HEADS):
        movt_ref[0, hd, 0:HEAD_DIM, :] = vt[hd * HEAD_DIM:(hd + 1) * HEAD_DIM, :].astype(BF16)
        movt_ref[0, hd, HEAD_DIM:, :] = jnp.ones((VT_ROWS - HEAD_DIM, vt.shape[1]), BF16)
    meq_ref[...] = (cq * QK_SCALE).astype(BF16)

    sb = jnp.concatenate([pa[:, 3 * MOBA_W + MEM_W:], pb, pc], axis=1)
    sbq_ref[...] = (sb[:, :SB_W] * QK_SCALE).astype(BF16)
    sbk_ref[...] = sb[:, SB_W:2 * SB_W].astype(BF16)
    sbv_ref[...] = sb[:, 2 * SB_W:].astype(BF16)


def _split_projection(w):
    sb = w[:, :3 * SB_W]
    rest = w[:, 3 * SB_W:]
    head = -rest.shape[1] % MXU_TILE
    mid = head + (sb.shape[1] - head) // (2 * MXU_TILE) * MXU_TILE
    return jnp.concatenate([rest, sb[:, :head]], axis=1), sb[:, head:mid], sb[:, mid:]


def _inproj(x2, g, ws, cos, sin, bd, gqk, gmq, seq):
    n, d = x2.shape
    tm = INPROJ_ROWS
    steps = n // tm
    per_seq = seq // tm
    blocks_per_step = tm // MOBA_BLOCK
    const = lambda i: (0, 0)
    rows = lambda i: (i, 0)
    out_shapes = [jax.ShapeDtypeStruct((n, SB_W), BF16)] * 3 + [
        jax.ShapeDtypeStruct((n // seq, MOBA_W, seq), BF16),
        jax.ShapeDtypeStruct((n, MOBA_W), BF16),
        jax.ShapeDtypeStruct((n // seq, MOBA_HEADS, VT_ROWS, seq), BF16),
        jax.ShapeDtypeStruct((n, MEM_W), BF16),
        jax.ShapeDtypeStruct((steps, blocks_per_step, MOBA_W), F32),
    ]
    out_specs = [pl.BlockSpec((tm, SB_W), rows)] * 3 + [
        pl.BlockSpec((1, MOBA_W, tm), lambda i: (i // per_seq, 0, i % per_seq)),
        pl.BlockSpec((tm, MOBA_W), rows),
        pl.BlockSpec((1, MOBA_HEADS, VT_ROWS, tm), lambda i: (i // per_seq, 0, 0, i % per_seq)),
        pl.BlockSpec((tm, MEM_W), rows),
        pl.BlockSpec((1, blocks_per_step, MOBA_W), lambda i: (i, 0, 0)),
    ]
    return pl.pallas_call(
        _inproj_kernel,
        grid=(steps,),
        in_specs=[
            pl.BlockSpec((tm, d), rows),
            pl.BlockSpec((1, d), const),
            *[pl.BlockSpec(w.shape, const) for w in ws],
            pl.BlockSpec((tm, LANES), lambda i: (i % per_seq, 0)),
            pl.BlockSpec((tm, LANES), lambda i: (i % per_seq, 0)),
            pl.BlockSpec((MXU_TILE, MXU_TILE), const),
            pl.BlockSpec((1, 2 * MOBA_W), const),
            pl.BlockSpec((1, MEM_W), const),
        ],
        out_specs=out_specs,
        out_shape=out_shapes,
        compiler_params=pltpu.CompilerParams(dimension_semantics=("arbitrary",), vmem_limit_bytes=VMEM_LIMIT),
        name="inproj",
    )(x2, g, *ws, cos, sin, bd, gqk, gmq)


def _sb_kernel(q_ref, k_ref, v_ref, o_ref):
    i = pl.program_id(1)
    lane = lax.broadcasted_iota(jnp.int32, (SB_SUB, LANES), 1)
    low = lane < HEAD_DIM
    kiota = lax.broadcasted_iota(jnp.int32, (SB_SUB, SB_WIN), 1)
    riota = lax.broadcasted_iota(jnp.int32, (SB_SUB, 1), 0)
    ur = lax.broadcasted_iota(jnp.int32, (SB_WIN, SB_WIN), 0)
    uc = lax.broadcasted_iota(jnp.int32, (SB_WIN, SB_WIN), 1)
    later = (ur > uc).astype(BF16)
    n_pairs = SB_W // LANES

    n_sub = SB_ROWS // SB_SUB
    chains = []
    for sub in range(n_sub):
        for pair in range(n_pairs):
            q2 = q_ref[0, sub * SB_SUB:(sub + 1) * SB_SUB, pair * LANES:(pair + 1) * LANES]
            zero = jnp.zeros_like(q2)
            chains += [(sub, pair, jnp.where(low, q2, zero)), (sub, pair, jnp.where(low, zero, q2))]
    n_chains = len(chains)

    def scan_windows(starts, limits, carries):
        def cols(c):
            return slice(chains[c][1] * LANES, (chains[c][1] + 1) * LANES)

        def scores(c):
            sub = chains[c][0]
            return _dot_nt(chains[c][2], k_ref[0, pl.ds(starts[sub], SB_WIN), cols(c)])

        valids = [(starts[sub] + kiota) < limits[sub] for sub in range(n_sub)]

        def scan(c, z):
            valid = valids[chains[c][0]]
            log_beta = jnp.minimum(z, 0.0) - jnp.log(1.0 + jnp.exp(-jnp.abs(z)))
            log_1m = jnp.where(valid, log_beta - z, 0.0)
            hi, lo = _split_bf16(log_1m)
            tail = _dot(hi, later) + _dot(lo, later)
            return log_beta, tail, carries[c] + jnp.sum(log_1m, axis=1, keepdims=True)

        def weigh(c, log_beta, tail):
            sub = chains[c][0]
            a = jnp.where(valids[sub], jnp.exp(log_beta + tail + carries[c]), 0.0)
            return _dot(a.astype(BF16), v_ref[0, pl.ds(starts[sub], SB_WIN), cols(c)])

        zs, scans, outs = {}, {}, [None] * n_chains
        for t in range(n_chains + 2 * SB_LOOKAHEAD):
            if t < n_chains:
                zs[t] = scores(t)
            c = t - SB_LOOKAHEAD
            if 0 <= c < n_chains:
                scans[c] = scan(c, zs.pop(c))
            c = t - 2 * SB_LOOKAHEAD
            if 0 <= c < n_chains:
                log_beta, tail, carry = scans.pop(c)
                outs[c] = (weigh(c, log_beta, tail), carry)
        return outs

    def unfinished(cs, starts):
        worst = None
        for c, carry in enumerate(cs):
            open_carry = jnp.where(starts[chains[c][0]] > 0, carry, NEG_BIG)
            worst = open_carry if worst is None else jnp.maximum(worst, open_carry)
        return (jnp.max(worst) > SB_DONE_LOG).astype(jnp.int32)

    t0s = [i * SB_ROWS + sub * SB_SUB for sub in range(n_sub)]
    zero_carries = [jnp.zeros((SB_SUB, 1), F32)] * n_chains

    def window_before(ends):
        starts = [pl.multiple_of(jnp.maximum(e - SB_WIN, 0), SB_SUB) for e in ends]
        limits = [jnp.minimum(t0 + riota, e) for t0, e in zip(t0s, ends)]
        return starts, limits

    def write(accs):
        for sub in range(n_sub):
            for pair in range(n_pairs):
                c = (sub * n_pairs + pair) * 2
                out = jnp.where(low, accs[c], accs[c + 1])
                o_ref[0, sub * SB_SUB:(sub + 1) * SB_SUB, pair * LANES:(pair + 1) * LANES] = out.astype(BF16)

    ends0 = [t0 + SB_SUB for t0 in t0s]
    starts0, limits0 = window_before(ends0)
    first = scan_windows(starts0, limits0, zero_carries)
    write([acc for acc, _ in first])
    @pl.when(unfinished([carry for _, carry in first], starts0) > 0)
    def _():
        def cond(state):
            ends, more = state[:n_sub], state[n_sub]
            return jnp.logical_and(ends[-1] > 0, more > 0)

        def body(state):
            ends = state[:n_sub]
            accs_c = state[n_sub + 1:n_sub + 1 + n_chains]
            cs_c = state[n_sub + 1 + n_chains:]
            starts, limits = window_before(ends)
            res = scan_windows(starts, limits, list(cs_c))
            new_cs = [r[1] for r in res]
            return (*starts, unfinished(new_cs, starts), *[a + r[0] for a, r in zip(accs_c, res)], *new_cs)

        zero_accs = [jnp.zeros((SB_SUB, LANES), F32)] * n_chains
        state = lax.while_loop(cond, body, (*ends0, jnp.int32(1), *zero_accs, *zero_carries))
        write(state[n_sub + 1:n_sub + 1 + n_chains])


def _sb_attention(q, k, v):
    b, s, w = q.shape
    return pl.pallas_call(
        _sb_kernel,
        grid=(b, s // SB_ROWS),
        in_specs=[
            pl.BlockSpec((1, SB_ROWS, w), lambda bi, i: (bi, i, 0)),
            pl.BlockSpec((1, s, w), lambda bi, i: (bi, 0, 0)),
            pl.BlockSpec((1, s, w), lambda bi, i: (bi, 0, 0)),
        ],
        out_specs=pl.BlockSpec((1, SB_ROWS, w), lambda bi, i: (bi, i, 0)),
        out_shape=jax.ShapeDtypeStruct((b, s, w), BF16),
        compiler_params=pltpu.CompilerParams(dimension_semantics=("arbitrary", "arbitrary"),
                                             vmem_limit_bytes=VMEM_LIMIT),
        name="sb_attention",
    )(q, k, v)


def _moba_kernel(qt_ref, k_ref, vt_ref, km_ref, gqk_ref, o_ref):
    first = pl.program_id(1) * MOBA_QBLOCKS
    nb = km_ref.shape[1]
    tq = MOBA_BLOCK
    n_heads = vt_ref.shape[1]
    blk = lax.broadcasted_iota(jnp.int32, (nb, tq), 0)
    items = [(qb, h) for qb in range(MOBA_QBLOCKS) for h in range(n_heads)]

    def cols(h):
        pair = h // 2
        return slice(pair * LANES, (pair + 1) * LANES)

    first_head_rows = lax.broadcasted_iota(jnp.int32, (LANES, MOBA_QBLOCKS * tq), 0) < HEAD_DIM
    qt_heads = []
    for pair in range(n_heads // 2):
        q2 = qt_ref[0, pair * LANES:(pair + 1) * LANES, :]
        zero = jnp.zeros_like(q2)
        qt_heads += [jnp.where(first_head_rows, q2, zero), jnp.where(first_head_rows, zero, q2)]
    qts = [qt_heads[h][:, qb * tq:(qb + 1) * tq] for qb, h in items]

    bits = [None] * len(items)

    def select_blocks():
        km_split = [_split_bf16(km_ref[0, :, cols(h)]) for h in range(n_heads)]
        gates = [_dot(km_split[h][0], qts[it]) + _dot(km_split[h][1], qts[it])
                 for it, (_, h) in enumerate(items)]
        blkf = blk.astype(F32)
        for it, (qb, _) in enumerate(items):
            gate = jnp.where(blk < first + qb, gates[it], -jnp.inf)
            chosen = jnp.zeros((1, tq), jnp.int32)
            for _ in range(MOBA_TOPK):
                best = jnp.max(gate, axis=0, keepdims=True)
                idx = jnp.min(jnp.where(gate == best, blkf, float(nb)), axis=0, keepdims=True)
                chosen = chosen | jnp.where(best > -jnp.inf, jnp.left_shift(1, idx.astype(jnp.int32)), 0)
                gate = jnp.where(blkf == idx, -jnp.inf, gate)
            bits[it] = chosen

    def block_offset(j):
        return pl.multiple_of(j * tq, tq)

    def pipelined(work, scores, reduce, lookahead=MOBA_LOOKAHEAD):
        queued = {n: scores(w) for n, w in enumerate(work[:lookahead])}
        for n, w in enumerate(work):
            if n + lookahead < len(work):
                queued[n + lookahead] = scores(work[n + lookahead])
            reduce(w, queued.pop(n))

    causal = lax.broadcasted_iota(jnp.int32, (tq, tq), 0) <= lax.broadcasted_iota(jnp.int32, (tq, tq), 1)
    n_items = len(items)
    all_items = list(range(n_items))

    def own_scores(it):
        qb, h = items[it]
        return _dot(k_ref[0, pl.ds(block_offset(first + qb), tq), cols(h)], qts[it])

    def past_scores(blocks_of):
        def scores(it):
            h = items[it][1]
            return [_dot(k_ref[0, pl.ds(block_offset(j), tq), cols(h)], qts[it]) for j in blocks_of(it)]
        return scores

    def v_rows(it, j):
        return vt_ref[0, items[it][1], :, pl.ds(block_offset(j), tq)]

    def picked(it, j):
        return (jnp.right_shift(bits[it], j) & 1) > 0

    def earlier_in_step(it):
        return [first + e for e in range(items[it][0])]

    def loop_blocks(jj):
        return lambda it: [MOBA_LOOP_BLOCKS * jj + u for u in range(MOBA_LOOP_BLOCKS)]

    loop_trips = first // MOBA_LOOP_BLOCKS

    def with_earlier(qb):
        return [it for it in all_items if items[it][0] == qb]

    def stable_path():
        state = [None] * (2 * n_items)
        select_blocks()

        def own_reduce(it, raw):
            s = jnp.where(causal, raw, NEG_BIG)
            m = jnp.max(s, axis=0, keepdims=True)
            state[2 * it] = m
            state[2 * it + 1] = _dot(v_rows(it, first + items[it][0]), jnp.exp(s - m).astype(BF16))

        pipelined(all_items, own_scores, own_reduce)

        def attend(st, work, blocks_of):
            def reduce(it, raws):
                m, acc = st[2 * it], st[2 * it + 1]
                picks = [picked(it, j) for j in blocks_of(it)]
                m_new = m
                for pick, s in zip(picks, raws):
                    m_new = jnp.maximum(m_new, jnp.where(pick, jnp.max(s, axis=0, keepdims=True), NEG_BIG))
                acc = jnp.exp(m - m_new) * acc
                for j, pick, s in zip(blocks_of(it), picks, raws):
                    p = jnp.exp((s - m_new).astype(BF16))
                    acc = acc + _dot(v_rows(it, j), jnp.where(pick, p, jnp.zeros_like(p)))
                st[2 * it] = m_new
                st[2 * it + 1] = acc
            pipelined(work, past_scores(blocks_of), reduce)

        for qb in range(1, MOBA_QBLOCKS):
            attend(state, with_earlier(qb), earlier_in_step)

        def body(jj, st):
            st = list(st)
            attend(st, all_items, loop_blocks(jj))
            return tuple(st)

        state = lax.fori_loop(0, loop_trips, body, tuple(state))
        return tuple(state[1::2])

    def bounded_path():
        def attend(st, blocks_of):
            work = []
            for it in all_items:
                blocks = blocks_of(it)
                work += [(it, blocks[c:c + MOBA_ENTRY_BLOCKS]) for c in range(0, len(blocks), MOBA_ENTRY_BLOCKS)]

            def scores(entry):
                it, blocks = entry
                return [_dot(k_ref[0, pl.ds(block_offset(j), tq), cols(items[it][1])], qts[it]) for j, _ in blocks]

            def reduce(entry, raws):
                it, blocks = entry
                for (j, is_own), s in zip(blocks, raws):
                    p = jnp.exp(s).astype(BF16)
                    p = jnp.where(causal if is_own else picked(it, j), p, jnp.zeros_like(p))
                    term = _dot(v_rows(it, j), p)
                    st[it] = term if st[it] is None else st[it] + term

            pipelined(work, scores, reduce, MOBA_ENTRY_LOOKAHEAD)

        select_blocks()
        accs = [None] * n_items
        attend(accs, lambda it: [(j, False) for j in earlier_in_step(it)] + [(first + items[it][0], True)])

        def body(jj, st):
            st = list(st)
            attend(st, lambda it: [(j, False) for j in loop_blocks(jj)(it)])
            return tuple(st)

        return lax.fori_loop(0, loop_trips, body, tuple(accs))

    gq = gqk_ref[:, :MOBA_W]
    gk = gqk_ref[:, MOBA_W:]
    score_bound_sq = (HEAD_DIM * SCORE_BOUND_MARGIN) * jnp.max(gq * gq) * jnp.max(gk * gk)
    accs = lax.cond(score_bound_sq <= MOBA_SAFE_LOGIT * MOBA_SAFE_LOGIT, bounded_path, stable_path)
    for qb in range(MOBA_QBLOCKS):
        for pair in range(n_heads // 2):
            outs = []
            for h in (2 * pair, 2 * pair + 1):
                acc = accs[qb * n_heads + h]
                outs.append(acc[:HEAD_DIM] / acc[HEAD_DIM:HEAD_DIM + 1])
            o_ref[0, qb * tq:(qb + 1) * tq, pair * LANES:(pair + 1) * LANES] = (
                jnp.concatenate(outs, axis=0).T.astype(BF16))


def _moba_attention(qt, k, vt, kmean, gqk):
    b, s, w = k.shape
    nb = s // MOBA_BLOCK
    rows = MOBA_QBLOCKS * MOBA_BLOCK
    return pl.pallas_call(
        _moba_kernel,
        grid=(b, nb // MOBA_QBLOCKS),
        in_specs=[
            pl.BlockSpec((1, w, rows), lambda bi, i: (bi, 0, i)),
            pl.BlockSpec((1, s, w), lambda bi, i: (bi, 0, 0)),
            pl.BlockSpec((1, w // HEAD_DIM, VT_ROWS, s), lambda bi, i: (bi, 0, 0, 0)),
            pl.BlockSpec((1, nb, w), lambda bi, i: (bi, 0, 0)),
            pl.BlockSpec((1, 2 * w), lambda bi, i: (0, 0)),
        ],
        out_specs=pl.BlockSpec((1, rows, w), lambda bi, i: (bi, i, 0)),
        out_shape=jax.ShapeDtypeStruct((b, s, w), BF16),
        compiler_params=pltpu.CompilerParams(dimension_semantics=("arbitrary",) * 2, vmem_limit_bytes=VMEM_LIMIT),
        name="moba_attention",
    )(qt, k, vt, kmean, gqk)


def _memattn_kernel(q_ref, ck_ref, cvt_ref, gq_ref, gk_ref, o_ref):
    tq = MXU_TILE
    lane = lax.broadcasted_iota(jnp.int32, (tq, LANES), 1)
    low = lane < HEAD_DIM
    work = [(tile, hd) for tile in range(q_ref.shape[1] // tq) for hd in range(MEM_HEADS)]

    def cols(hd):
        return slice((hd // 2) * LANES, (hd // 2 + 1) * LANES)

    def scores(w):
        tile, hd = work[w]
        q2 = q_ref[0, tile * tq:(tile + 1) * tq, cols(hd)]
        zero = jnp.zeros_like(q2)
        qm = jnp.where(low, q2, zero) if hd % 2 == 0 else jnp.where(low, zero, q2)
        return _dot_nt(ck_ref[0, :, cols(hd)], qm)

    def attend(subtract_max):
        queued = {w: scores(w) for w in range(MEMATTN_LOOKAHEAD)}
        outs = {}
        for w, (tile, hd) in enumerate(work):
            if w + MEMATTN_LOOKAHEAD < len(work):
                queued[w + MEMATTN_LOOKAHEAD] = scores(w + MEMATTN_LOOKAHEAD)
            s = queued.pop(w)
            if subtract_max:
                s = s - jnp.max(s, axis=0, keepdims=True)
            acc = _dot(cvt_ref[0, hd], jnp.exp(s).astype(BF16))
            outs[hd % 2] = acc[:HEAD_DIM] / acc[HEAD_DIM:HEAD_DIM + 1]
            if hd % 2 == 1:
                o_ref[0, tile * tq:(tile + 1) * tq, cols(hd)] = (
                    jnp.concatenate([outs[0], outs[1]], axis=0).T.astype(BF16))

    gq, gk = gq_ref[...], gk_ref[...]
    bounded = (HEAD_DIM * SCORE_BOUND_MARGIN) * jnp.max(gq * gq) * jnp.max(gk * gk) <= MOBA_SAFE_LOGIT ** 2
    pl.when(bounded)(lambda: attend(False))
    pl.when(jnp.logical_not(bounded))(lambda: attend(True))


def _mem_attention(q, ck, cv, gq, gk):
    b, s, w = q.shape
    m = ck.shape[1]
    tm = MEMATTN_ROWS
    return pl.pallas_call(
        _memattn_kernel,
        grid=(b, s // tm),
        in_specs=[
            pl.BlockSpec((1, tm, w), lambda bi, i: (bi, i, 0)),
            pl.BlockSpec((1, m, w), lambda bi, i: (bi, 0, 0)),
            pl.BlockSpec((1, MEM_HEADS, VT_ROWS, m), lambda bi, i: (bi, 0, 0, 0)),
            pl.BlockSpec((1, w), lambda bi, i: (0, 0)),
            pl.BlockSpec((1, w), lambda bi, i: (0, 0)),
        ],
        out_specs=pl.BlockSpec((1, tm, w), lambda bi, i: (bi, i, 0)),
        out_shape=jax.ShapeDtypeStruct((b, s, w), BF16),
        compiler_params=pltpu.CompilerParams(dimension_semantics=("arbitrary", "arbitrary"),
                                             vmem_limit_bytes=VMEM_LIMIT),
        name="mem_attention",
    )(q, ck, cv, gq, gk)


def _sigmoid(t):
    return 1.0 / (1.0 + jnp.exp(-t))


def _tail_kernel(x_ref, osb_ref, omo_ref, ome_ref, gmix_ref, gffn_ref, wg_ref, wsb_ref, wmo_ref, wme_ref,
                 wout_ref, wfi_ref, wfd_ref, out_ref):
    d = x_ref.shape[1]
    dff = wfd_ref.shape[0]
    half = x_ref.shape[0] // 2

    def merged(rows):
        x = x_ref[rows, :]
        h = _rms_rows(x, gmix_ref[...]).astype(BF16)
        mix = None
        for n, (o_ref, wu_ref) in enumerate(((osb_ref, wsb_ref), (omo_ref, wmo_ref), (ome_ref, wme_ref))):
            gate = _dot(h, wg_ref[:, n * d:(n + 1) * d])
            term = _sigmoid(gate) * _dot(o_ref[rows, :], wu_ref[...])
            mix = term if mix is None else mix + term
        return x + _dot(mix.astype(BF16), wout_ref[...])

    def swiglu(x1):
        h2 = _rms_rows(x1, gffn_ref[...]).astype(BF16)
        gate = _dot(h2, wfi_ref[:, :dff])
        up = _dot(h2, wfi_ref[:, dff:])
        ff = (gate * _sigmoid(gate) * up).astype(BF16)
        return x1 + _dot(ff, wfd_ref[...])

    halves = [slice(r * half, (r + 1) * half) for r in range(2)]
    x1s = [merged(rows) for rows in halves]
    for rows, x1 in zip(halves, x1s):
        out_ref[rows, :] = swiglu(x1)


def _tail(x2, osb, omo, ome, gmix, gffn, wg, wsb, wmo, wme, wout, wfi, wfd):
    n, d = x2.shape
    tm = TAIL_ROWS
    rows = lambda i: (i, 0)

    def resident(arr):
        return pl.BlockSpec(arr.shape, lambda i: (0, 0), pipeline_mode=pl.Buffered(1))

    return pl.pallas_call(
        _tail_kernel,
        grid=(n // tm,),
        in_specs=[
            pl.BlockSpec((tm, d), rows),
            pl.BlockSpec((tm, SB_W), rows),
            pl.BlockSpec((tm, MOBA_W), rows),
            pl.BlockSpec((tm, MEM_W), rows),
            resident(gmix), resident(gffn), resident(wg), resident(wsb), resident(wmo), resident(wme),
            resident(wout), resident(wfi), resident(wfd),
        ],
        out_specs=pl.BlockSpec((tm, d), rows),
        out_shape=jax.ShapeDtypeStruct((n, d), F32),
        compiler_params=pltpu.CompilerParams(dimension_semantics=("arbitrary",), vmem_limit_bytes=VMEM_LIMIT),
        name="merge_out_ffn",
    )(x2, osb, omo, ome, gmix, gffn, wg, wsb, wmo, wme, wout, wfi, wfd)


def _rope_tables(seq):
    half = HEAD_DIM // 2
    inv_freq = ROPE_THETA ** (-jnp.arange(half, dtype=F32) * 2.0 / HEAD_DIM)
    ang = jnp.arange(seq, dtype=F32)[:, None] * inv_freq[None, :]
    cos, sin = jnp.cos(ang), jnp.sin(ang)
    reps = LANES // HEAD_DIM
    return jnp.tile(jnp.concatenate([cos, cos], axis=1), (1, reps)), jnp.tile(jnp.concatenate([-sin, sin], axis=1), (1, reps))


def _head_mean_matrix(width):
    r = jnp.arange(width) // HEAD_DIM
    return jnp.where(r[:, None] == r[None, :], 1.0 / HEAD_DIM, 0.0).astype(BF16)


def _layer(x, mem, mix_norm_g, mem_norm_g, ffn_norm_g, w_in, w_mem_kv, moba_q_norm_g, moba_k_norm_g,
           mem_q_norm_g, mem_k_norm_g, w_up_sb, w_up_moba, w_up_mem, w_out, w_ffn_in, w_ffn_down):
    b, s, d = x.shape
    x2 = x.reshape(b * s, d)
    cos, sin = _rope_tables(s)
    bd = _head_mean_matrix(MXU_TILE)
    row = lambda g, reps=1: jnp.tile(g.astype(F32), reps)[None, :]
    gqk = jnp.concatenate([row(moba_q_norm_g, MOBA_HEADS), row(moba_k_norm_g, MOBA_HEADS)], axis=1)

    ck, cv = _memkv(mem, row(mem_norm_g), w_mem_kv.astype(BF16), bd, row(mem_k_norm_g, MEM_HEADS))
    sbq, sbk, sbv, moqt, mok, movt, meq, kmean = _inproj(
        x2, row(mix_norm_g), _split_projection(w_in[:, :QKV_W].astype(BF16)), cos, sin, bd, gqk,
        row(mem_q_norm_g, MEM_HEADS), s)
    seq3 = lambda t: t.reshape(b, s, t.shape[-1])
    o_sb = _sb_attention(seq3(sbq), seq3(sbk), seq3(sbv))
    o_mo = _moba_attention(moqt, seq3(mok), movt, kmean.reshape(b, s // MOBA_BLOCK, MOBA_W), gqk)
    o_me = _mem_attention(seq3(meq), ck, cv, row(mem_q_norm_g, MEM_HEADS), row(mem_k_norm_g, MEM_HEADS))
    flat = lambda t: t.reshape(b * s, t.shape[-1])
    out = _tail(x2, flat(o_sb), flat(o_mo), flat(o_me), row(mix_norm_g), row(ffn_norm_g),
                w_in[:, QKV_W:].astype(BF16), w_up_sb.astype(BF16), w_up_moba.astype(BF16), w_up_mem.astype(BF16),
                w_out.astype(BF16), w_ffn_in.astype(BF16), w_ffn_down.astype(BF16))
    return out.reshape(b, s, d)


def kernel(x, mem, mix_norm_g, mem_norm_g, ffn_norm_g, w_in, w_mem_kv, moba_q_norm_g, moba_k_norm_g, mem_q_norm_g, mem_k_norm_g, w_up_sb, w_up_moba, w_up_mem, w_out, w_ffn_in, w_ffn_down):
    for l in range(w_in.shape[0]):
        x = _layer(x, mem, mix_norm_g[l], mem_norm_g[l], ffn_norm_g[l], w_in[l], w_mem_kv[l],
                   moba_q_norm_g[l], moba_k_norm_g[l], mem_q_norm_g[l], mem_k_norm_g[l],
                   w_up_sb[l], w_up_moba[l], w_up_mem[l], w_out[l], w_ffn_in[l], w_ffn_down[l])
    return x
```

```python
import math

import jax
import jax.numpy as jnp
from jax import lax
from jax.experimental import pallas as pl
from jax.experimental.pallas import tpu as pltpu

F32 = jnp.float32
BF16 = jnp.bfloat16

HEAD_DIM = 64
SB_HEADS = 6
MOBA_HEADS = 6
MEM_HEADS = 4
SB_W = SB_HEADS * HEAD_DIM
MOBA_W = MOBA_HEADS * HEAD_DIM
MEM_W = MEM_HEADS * HEAD_DIM
QKV_W = 3 * SB_W + 3 * MOBA_W + MEM_W
MOBA_BLOCK = 256
MOBA_TOPK = 3
ROPE_THETA = 10000.0
EPS = 1e-6
QK_SCALE = 1.0 / math.sqrt(HEAD_DIM)

LANES = 128
MXU_TILE = 256
BF16_SUBLANES = 16
VT_ROWS = HEAD_DIM + BF16_SUBLANES
NEG_BIG = -1e30

SB_DONE_LOG = -30.0

INPROJ_ROWS = 1024
SB_ROWS = 1024
SB_SUB = 64
SB_WIN = 128
SB_LOOKAHEAD = 10
MOBA_LOOKAHEAD = 2
MOBA_QBLOCKS = 4
MOBA_LOOP_BLOCKS = 4
MOBA_ENTRY_BLOCKS = 2
MOBA_ENTRY_LOOKAHEAD = 3
MOBA_SAFE_LOGIT = 40.0
SCORE_BOUND_MARGIN = 1.1
MEMATTN_ROWS = 2048
MEMATTN_LOOKAHEAD = 5
TAIL_ROWS = 512
VMEM_LIMIT = 56 * 1024 * 1024


def _dot(a, b):
    return jnp.dot(a, b, preferred_element_type=F32)


def _dot_nt(a, b):
    return lax.dot_general(a, b, (((1,), (1,)), ((), ())), preferred_element_type=F32)


def _split_bf16(t):
    hi = t.astype(BF16)
    lo = (t - hi.astype(F32)).astype(BF16)
    return hi, lo


def _rms_rows(x, g):
    ms = jnp.mean(x * x, axis=-1, keepdims=True)
    return x * lax.rsqrt(ms + EPS) * g


def _head_rms(t, bd, g):
    hi, lo = _split_bf16(t * t)
    parts = []
    for c in range(0, t.shape[1], MXU_TILE):
        w = min(MXU_TILE, t.shape[1] - c)
        parts.append(_dot(hi[:, c:c + w], bd[:w, :w]) + _dot(lo[:, c:c + w], bd[:w, :w]))
    ms = parts[0] if len(parts) == 1 else jnp.concatenate(parts, axis=1)
    return t * lax.rsqrt(ms + EPS) * g


def _rotate_half_pairs(t):
    lane = lax.broadcasted_iota(jnp.int32, t.shape, 1)
    first_half = (lane & (HEAD_DIM - 1)) < (HEAD_DIM // 2)
    return jnp.where(first_half, pltpu.roll(t, LANES - HEAD_DIM // 2, 1), pltpu.roll(t, HEAD_DIM // 2, 1))


def _rope(t, cos, sin_signed):
    outs = []
    for c in range(t.shape[1] // LANES):
        tc = t[:, c * LANES:(c + 1) * LANES]
        outs.append(tc * cos + _rotate_half_pairs(tc) * sin_signed)
    return jnp.concatenate(outs, axis=1)


def _memkv_kernel(mem_ref, g_ref, w_ref, bd_ref, gk_ref, ck_ref, cvt_ref):
    h = _rms_rows(mem_ref[0], g_ref[...]).astype(BF16)
    kv = _dot(h, w_ref[...])
    k = kv[:, :MEM_W]
    ck_ref[0] = _head_rms(k, bd_ref[...], gk_ref[...]).astype(BF16)
    vt = kv[:, MEM_W:].T
    for hd in range(MEM_HEADS):
        cvt_ref[0, hd, 0:HEAD_DIM, :] = vt[hd * HEAD_DIM:(hd + 1) * HEAD_DIM, :].astype(BF16)
        cvt_ref[0, hd, HEAD_DIM:, :] = jnp.ones((VT_ROWS - HEAD_DIM, vt.shape[1]), BF16)


def _memkv(mem, g, w, bd, gk):
    b, m, d = mem.shape
    return pl.pallas_call(
        _memkv_kernel,
        grid=(b,),
        in_specs=[
            pl.BlockSpec((1, m, d), lambda i: (i, 0, 0)),
            pl.BlockSpec((1, d), lambda i: (0, 0)),
            pl.BlockSpec((d, 2 * MEM_W), lambda i: (0, 0)),
            pl.BlockSpec((MXU_TILE, MXU_TILE), lambda i: (0, 0)),
            pl.BlockSpec((1, MEM_W), lambda i: (0, 0)),
        ],
        out_specs=[
            pl.BlockSpec((1, m, MEM_W), lambda i: (i, 0, 0)),
            pl.BlockSpec((1, MEM_HEADS, VT_ROWS, m), lambda i: (i, 0, 0, 0)),
        ],
        out_shape=[jax.ShapeDtypeStruct((b, m, MEM_W), BF16),
                   jax.ShapeDtypeStruct((b, MEM_HEADS, VT_ROWS, m), BF16)],
        compiler_params=pltpu.CompilerParams(dimension_semantics=("arbitrary",), vmem_limit_bytes=VMEM_LIMIT),
        name="memkv",
    )(mem, g, w, bd, gk)


def _inproj_kernel(x_ref, g_ref, wa_ref, wb_ref, wc_ref, cos_ref, sin_ref, bd_ref, gqk_ref, gmq_ref,
                   sbq_ref, sbk_ref, sbv_ref, moqt_ref, mok_ref, movt_ref, meq_ref, kmean_ref):
    h = _rms_rows(x_ref[...], g_ref[...]).astype(BF16)
    pa = _dot(h, wa_ref[...])
    pb = _dot(h, wb_ref[...])
    bd = bd_ref[...]
    qk_normed = _head_rms(pa[:, :2 * MOBA_W], bd, gqk_ref[...])
    cq = _head_rms(pa[:, 3 * MOBA_W:3 * MOBA_W + MEM_W], bd, gmq_ref[...])
    pc = _dot(h, wc_ref[...])

    qk = _rope(qk_normed, cos_ref[...], sin_ref[...])
    moqt_ref[0] = (qk[:, :MOBA_W] * QK_SCALE).T.astype(BF16)
    mk = qk[:, MOBA_W:]
    mok_ref[...] = mk.astype(BF16)
    for blk in range(INPROJ_ROWS // MOBA_BLOCK):
        kmean_ref[0, blk:blk + 1, :] = jnp.mean(mk[blk * MOBA_BLOCK:(blk + 1) * MOBA_BLOCK], axis=0, keepdims=True)
    vt = pa[:, 2 * MOBA_W:3 * MOBA_W].T
    for hd in range(MOBA_HEADS):
        movt_ref[0, hd, 0:HEAD_DIM, :] = vt[hd * HEAD_DIM:(hd + 1) * HEAD_DIM, :].astype(BF16)
        movt_ref[0, hd, HEAD_DIM:, :] = jnp.ones((VT_ROWS - HEAD_DIM, vt.shape[1]), BF16)
    meq_ref[...] = (cq * QK_SCALE).astype(BF16)

    sb = jnp.concatenate([pa[:, 3 * MOBA_W + MEM_W:], pb, pc], axis=1)
    sbq_ref[...] = (sb[:, :SB_W] * QK_SCALE).astype(BF16)
    sbk_ref[...] = sb[:, SB_W:2 * SB_W].astype(BF16)
    sbv_ref[...] = sb[:, 2 * SB_W:].astype(BF16)


def _split_projection(w):
    sb = w[:, :3 * SB_W]
    rest = w[:, 3 * SB_W:]
    head = -rest.shape[1] % MXU_TILE
    mid = head + (sb.shape[1] - head) // (2 * MXU_TILE) * MXU_TILE
    return jnp.concatenate([rest, sb[:, :head]], axis=1), sb[:, head:mid], sb[:, mid:]


def _inproj(x2, g, ws, cos, sin, bd, gqk, gmq, seq):
    n, d = x2.shape
    tm = INPROJ_ROWS
    steps = n // tm
    per_seq = seq // tm
    blocks_per_step = tm // MOBA_BLOCK
    const = lambda i: (0, 0)
    rows = lambda i: (i, 0)
    out_shapes = [jax.ShapeDtypeStruct((n, SB_W), BF16)] * 3 + [
        jax.ShapeDtypeStruct((n // seq, MOBA_W, seq), BF16),
        jax.ShapeDtypeStruct((n, MOBA_W), BF16),
        jax.ShapeDtypeStruct((n // seq, MOBA_HEADS, VT_ROWS, seq), BF16),
        jax.ShapeDtypeStruct((n, MEM_W), BF16),
        jax.ShapeDtypeStruct((steps, blocks_per_step, MOBA_W), F32),
    ]
    out_specs = [pl.BlockSpec((tm, SB_W), rows)] * 3 + [
        pl.BlockSpec((1, MOBA_W, tm), lambda i: (i // per_seq, 0, i % per_seq)),
        pl.BlockSpec((tm, MOBA_W), rows),
        pl.BlockSpec((1, MOBA_HEADS, VT_ROWS, tm), lambda i: (i // per_seq, 0, 0, i % per_seq)),
        pl.BlockSpec((tm, MEM_W), rows),
        pl.BlockSpec((1, blocks_per_step, MOBA_W), lambda i: (i, 0, 0)),
    ]
    return pl.pallas_call(
        _inproj_kernel,
        grid=(steps,),
        in_specs=[
            pl.BlockSpec((tm, d), rows),
            pl.BlockSpec((1, d), const),
            *[pl.BlockSpec(w.shape, const) for w in ws],
            pl.BlockSpec((tm, LANES), lambda i: (i % per_seq, 0)),
            pl.BlockSpec((tm, LANES), lambda i: (i % per_seq, 0)),
            pl.BlockSpec((MXU_TILE, MXU_TILE), const),
            pl.BlockSpec((1, 2 * MOBA_W), const),
            pl.BlockSpec((1, MEM_W), const),
        ],
        out_specs=out_specs,
        out_shape=out_shapes,
        compiler_params=pltpu.CompilerParams(dimension_semantics=("arbitrary",), vmem_limit_bytes=VMEM_LIMIT),
        name="inproj",
    )(x2, g, *ws, cos, sin, bd, gqk, gmq)


def _sb_kernel(q_ref, k_ref, v_ref, o_ref):
    i = pl.program_id(1)
    lane = lax.broadcasted_iota(jnp.int32, (SB_SUB, LANES), 1)
    low = lane < HEAD_DIM
    kiota = lax.broadcasted_iota(jnp.int32, (SB_SUB, SB_WIN), 1)
    riota = lax.broadcasted_iota(jnp.int32, (SB_SUB, 1), 0)
    ur = lax.broadcasted_iota(jnp.int32, (SB_WIN, SB_WIN), 0)
    uc = lax.broadcasted_iota(jnp.int32, (SB_WIN, SB_WIN), 1)
    later = (ur > uc).astype(BF16)
    n_pairs = SB_W // LANES

    n_sub = SB_ROWS // SB_SUB
    chains = []
    for sub in range(n_sub):
        for pair in range(n_pairs):
            q2 = q_ref[0, sub * SB_SUB:(sub + 1) * SB_SUB, pair * LANES:(pair + 1) * LANES]
            zero = jnp.zeros_like(q2)
            chains += [(sub, pair, jnp.where(low, q2, zero)), (sub, pair, jnp.where(low, zero, q2))]
    n_chains = len(chains)

    def scan_windows(starts, limits, carries):
        def cols(c):
            return slice(chains[c][1] * LANES, (chains[c][1] + 1) * LANES)

        def scores(c):
            sub = chains[c][0]
            return _dot_nt(chains[c][2], k_ref[0, pl.ds(starts[sub], SB_WIN), cols(c)])

        valids = [(starts[sub] + kiota) < limits[sub] for sub in range(n_sub)]

        def scan(c, z):
            valid = valids[chains[c][0]]
            log_beta = jnp.minimum(z, 0.0) - jnp.log(1.0 + jnp.exp(-jnp.abs(z)))
            log_1m = jnp.where(valid, log_beta - z, 0.0)
            hi, lo = _split_bf16(log_1m)
            tail = _dot(hi, later) + _dot(lo, later)
            return log_beta, tail, carries[c] + jnp.sum(log_1m, axis=1, keepdims=True)

        def weigh(c, log_beta, tail):
            sub = chains[c][0]
            a = jnp.where(valids[sub], jnp.exp(log_beta + tail + carries[c]), 0.0)
            return _dot(a.astype(BF16), v_ref[0, pl.ds(starts[sub], SB_WIN), cols(c)])

        zs, scans, outs = {}, {}, [None] * n_chains
        for t in range(n_chains + 2 * SB_LOOKAHEAD):
            if t < n_chains:
                zs[t] = scores(t)
            c = t - SB_LOOKAHEAD
            if 0 <= c < n_chains:
                scans[c] = scan(c, zs.pop(c))
            c = t - 2 * SB_LOOKAHEAD
            if 0 <= c < n_chains:
                log_beta, tail, carry = scans.pop(c)
                outs[c] = (weigh(c, log_beta, tail), carry)
        return outs

    def unfinished(cs, starts):
        worst = None
        for c, carry in enumerate(cs):
            open_carry = jnp.where(starts[chains[c][0]] > 0, carry, NEG_BIG)
            worst = open_carry if worst is None else jnp.maximum(worst, open_carry)
        return (jnp.max(worst) > SB_DONE_LOG).astype(jnp.int32)

    t0s = [i * SB_ROWS + sub * SB_SUB for sub in range(n_sub)]
    zero_carries = [jnp.zeros((SB_SUB, 1), F32)] * n_chains

    def window_before(ends):
        starts = [pl.multiple_of(jnp.maximum(e - SB_WIN, 0), SB_SUB) for e in ends]
        limits = [jnp.minimum(t0 + riota, e) for t0, e in zip(t0s, ends)]
        return starts, limits

    def write(accs):
        for sub in range(n_sub):
            for pair in range(n_pairs):
                c = (sub * n_pairs + pair) * 2
                out = jnp.where(low, accs[c], accs[c + 1])
                o_ref[0, sub * SB_SUB:(sub + 1) * SB_SUB, pair * LANES:(pair + 1) * LANES] = out.astype(BF16)

    ends0 = [t0 + SB_SUB for t0 in t0s]
    starts0, limits0 = window_before(ends0)
    first = scan_windows(starts0, limits0, zero_carries)
    write([acc for acc, _ in first])
    @pl.when(unfinished([carry for _, carry in first], starts0) > 0)
    def _():
        def cond(state):
            ends, more = state[:n_sub], state[n_sub]
            return jnp.logical_and(ends[-1] > 0, more > 0)

        def body(state):
            ends = state[:n_sub]
            accs_c = state[n_sub + 1:n_sub + 1 + n_chains]
            cs_c = state[n_sub + 1 + n_chains:]
            starts, limits = window_before(ends)
            res = scan_windows(starts, limits, list(cs_c))
            new_cs = [r[1] for r in res]
            return (*starts, unfinished(new_cs, starts), *[a + r[0] for a, r in zip(accs_c, res)], *new_cs)

        zero_accs = [jnp.zeros((SB_SUB, LANES), F32)] * n_chains
        state = lax.while_loop(cond, body, (*ends0, jnp.int32(1), *zero_accs, *zero_carries))
        write(state[n_sub + 1:n_sub + 1 + n_chains])


def _sb_attention(q, k, v):
    b, s, w = q.shape
    return pl.pallas_call(
        _sb_kernel,
        grid=(b, s // SB_ROWS),
        in_specs=[
            pl.BlockSpec((1, SB_ROWS, w), lambda bi, i: (bi, i, 0)),
            pl.BlockSpec((1, s, w), lambda bi, i: (bi, 0, 0)),
            pl.BlockSpec((1, s, w), lambda bi, i: (bi, 0, 0)),
        ],
        out_specs=pl.BlockSpec((1, SB_ROWS, w), lambda bi, i: (bi, i, 0)),
        out_shape=jax.ShapeDtypeStruct((b, s, w), BF16),
        compiler_params=pltpu.CompilerParams(dimension_semantics=("arbitrary", "arbitrary"),
                                             vmem_limit_bytes=VMEM_LIMIT),
        name="sb_attention",
    )(q, k, v)


def _moba_kernel(qt_ref, k_ref, vt_ref, km_ref, gqk_ref, o_ref):
    first = pl.program_id(1) * MOBA_QBLOCKS
    nb = km_ref.shape[1]
    tq = MOBA_BLOCK
    n_heads = vt_ref.shape[1]
    blk = lax.broadcasted_iota(jnp.int32, (nb, tq), 0)
    items = [(qb, h) for qb in range(MOBA_QBLOCKS) for h in range(n_heads)]

    def cols(h):
        pair = h // 2
        return slice(pair * LANES, (pair + 1) * LANES)

    first_head_rows = lax.broadcasted_iota(jnp.int32, (LANES, MOBA_QBLOCKS * tq), 0) < HEAD_DIM
    qt_heads = []
    for pair in range(n_heads // 2):
        q2 = qt_ref[0, pair * LANES:(pair + 1) * LANES, :]
        zero = jnp.zeros_like(q2)
        qt_heads += [jnp.where(first_head_rows, q2, zero), jnp.where(first_head_rows, zero, q2)]
    qts = [qt_heads[h][:, qb * tq:(qb + 1) * tq] for qb, h in items]

    bits = [None] * len(items)

    def select_blocks():
        km_split = [_split_bf16(km_ref[0, :, cols(h)]) for h in range(n_heads)]
        gates = [_dot(km_split[h][0], qts[it]) + _dot(km_split[h][1], qts[it])
                 for it, (_, h) in enumerate(items)]
        blkf = blk.astype(F32)
        for it, (qb, _) in enumerate(items):
            gate = jnp.where(blk < first + qb, gates[it], -jnp.inf)
            chosen = jnp.zeros((1, tq), jnp.int32)
            for _ in range(MOBA_TOPK):
                best = jnp.max(gate, axis=0, keepdims=True)
                idx = jnp.min(jnp.where(gate == best, blkf, float(nb)), axis=0, keepdims=True)
                chosen = chosen | jnp.where(best > -jnp.inf, jnp.left_shift(1, idx.astype(jnp.int32)), 0)
                gate = jnp.where(blkf == idx, -jnp.inf, gate)
            bits[it] = chosen

    def block_offset(j):
        return pl.multiple_of(j * tq, tq)

    def pipelined(work, scores, reduce, lookahead=MOBA_LOOKAHEAD):
        queued = {n: scores(w) for n, w in enumerate(work[:lookahead])}
        for n, w in enumerate(work):
            if n + lookahead < len(work):
                queued[n + lookahead] = scores(work[n + lookahead])
            reduce(w, queued.pop(n))

    causal = lax.broadcasted_iota(jnp.int32, (tq, tq), 0) <= lax.broadcasted_iota(jnp.int32, (tq, tq), 1)
    n_items = len(items)
    all_items = list(range(n_items))

    def own_scores(it):
        qb, h = items[it]
        return _dot(k_ref[0, pl.ds(block_offset(first + qb), tq), cols(h)], qts[it])

    def past_scores(blocks_of):
        def scores(it):
            h = items[it][1]
            return [_dot(k_ref[0, pl.ds(block_offset(j), tq), cols(h)], qts[it]) for j in blocks_of(it)]
        return scores

    def v_rows(it, j):
        return vt_ref[0, items[it][1], :, pl.ds(block_offset(j), tq)]

    def picked(it, j):
        return (jnp.right_shift(bits[it], j) & 1) > 0

    def earlier_in_step(it):
        return [first + e for e in range(items[it][0])]

    def loop_blocks(jj):
        return lambda it: [MOBA_LOOP_BLOCKS * jj + u for u in range(MOBA_LOOP_BLOCKS)]

    loop_trips = first // MOBA_LOOP_BLOCKS

    def with_earlier(qb):
        return [it for it in all_items if items[it][0] == qb]

    def stable_path():
        state = [None] * (2 * n_items)
        select_blocks()

        def own_reduce(it, raw):
            s = jnp.where(causal, raw, NEG_BIG)
            m = jnp.max(s, axis=0, keepdims=True)
            state[2 * it] = m
            state[2 * it + 1] = _dot(v_rows(it, first + items[it][0]), jnp.exp(s - m).astype(BF16))

        pipelined(all_items, own_scores, own_reduce)

        def attend(st, work, blocks_of):
            def reduce(it, raws):
                m, acc = st[2 * it], st[2 * it + 1]
                picks = [picked(it, j) for j in blocks_of(it)]
                m_new = m
                for pick, s in zip(picks, raws):
                    m_new = jnp.maximum(m_new, jnp.where(pick, jnp.max(s, axis=0, keepdims=True), NEG_BIG))
                acc = jnp.exp(m - m_new) * acc
                for j, pick, s in zip(blocks_of(it), picks, raws):
                    p = jnp.exp((s - m_new).astype(BF16))
                    acc = acc + _dot(v_rows(it, j), jnp.where(pick, p, jnp.zeros_like(p)))
                st[2 * it] = m_new
                st[2 * it + 1] = acc
            pipelined(work, past_scores(blocks_of), reduce)

        for qb in range(1, MOBA_QBLOCKS):
            attend(state, with_earlier(qb), earlier_in_step)

        def body(jj, st):
            st = list(st)
            attend(st, all_items, loop_blocks(jj))
            return tuple(st)

        state = lax.fori_loop(0, loop_trips, body, tuple(state))
        return tuple(state[1::2])

    def bounded_path():
        def attend(st, blocks_of):
            work = []
            for it in all_items:
                blocks = blocks_of(it)
                work += [(it, blocks[c:c + MOBA_ENTRY_BLOCKS]) for c in range(0, len(blocks), MOBA_ENTRY_BLOCKS)]

            def scores(entry):
                it, blocks = entry
                return [_dot(k_ref[0, pl.ds(block_offset(j), tq), cols(items[it][1])], qts[it]) for j, _ in blocks]

            def reduce(entry, raws):
                it, blocks = entry
                for (j, is_own), s in zip(blocks, raws):
                    p = jnp.exp(s.astype(BF16))
                    p = jnp.where(causal if is_own else picked(it, j), p, jnp.zeros_like(p))
                    term = _dot(v_rows(it, j), p)
                    st[it] = term if st[it] is None else st[it] + term

            pipelined(work, scores, reduce, MOBA_ENTRY_LOOKAHEAD)

        select_blocks()
        accs = [None] * n_items
        attend(accs, lambda it: [(j, False) for j in earlier_in_step(it)] + [(first + items[it][0], True)])

        def body(jj, st):
            st = list(st)
            attend(st, lambda it: [(j, False) for j in loop_blocks(jj)(it)])
            return tuple(st)

        return lax.fori_loop(0, loop_trips, body, tuple(accs))

    gq = gqk_ref[:, :MOBA_W]
    gk = gqk_ref[:, MOBA_W:]
    score_bound_sq = (HEAD_DIM * SCORE_BOUND_MARGIN) * jnp.max(gq * gq) * jnp.max(gk * gk)
    accs = lax.cond(score_bound_sq <= MOBA_SAFE_LOGIT * MOBA_SAFE_LOGIT, bounded_path, stable_path)
    for qb in range(MOBA_QBLOCKS):
        for pair in range(n_heads // 2):
            outs = []
            for h in (2 * pair, 2 * pair + 1):
                acc = accs[qb * n_heads + h]
                outs.append(acc[:HEAD_DIM] / acc[HEAD_DIM:HEAD_DIM + 1])
            o_ref[0, qb * tq:(qb + 1) * tq, pair * LANES:(pair + 1) * LANES] = (
                jnp.concatenate(outs, axis=0).T.astype(BF16))


def _moba_attention(qt, k, vt, kmean, gqk):
    b, s, w = k.shape
    nb = s // MOBA_BLOCK
    rows = MOBA_QBLOCKS * MOBA_BLOCK
    return pl.pallas_call(
        _moba_kernel,
        grid=(b, nb // MOBA_QBLOCKS),
        in_specs=[
            pl.BlockSpec((1, w, rows), lambda bi, i: (bi, 0, i)),
            pl.BlockSpec((1, s, w), lambda bi, i: (bi, 0, 0)),
            pl.BlockSpec((1, w // HEAD_DIM, VT_ROWS, s), lambda bi, i: (bi, 0, 0, 0)),
            pl.BlockSpec((1, nb, w), lambda bi, i: (bi, 0, 0)),
            pl.BlockSpec((1, 2 * w), lambda bi, i: (0, 0)),
        ],
        out_specs=pl.BlockSpec((1, rows, w), lambda bi, i: (bi, i, 0)),
        out_shape=jax.ShapeDtypeStruct((b, s, w), BF16),
        compiler_params=pltpu.CompilerParams(dimension_semantics=("arbitrary",) * 2, vmem_limit_bytes=VMEM_LIMIT),
        name="moba_attention",
    )(qt, k, vt, kmean, gqk)


def _memattn_kernel(q_ref, ck_ref, cvt_ref, gq_ref, gk_ref, o_ref):
    tq = MXU_TILE
    lane = lax.broadcasted_iota(jnp.int32, (tq, LANES), 1)
    low = lane < HEAD_DIM
    work = [(tile, hd) for tile in range(q_ref.shape[1] // tq) for hd in range(MEM_HEADS)]

    def cols(hd):
        return slice((hd // 2) * LANES, (hd // 2 + 1) * LANES)

    def scores(w):
        tile, hd = work[w]
        q2 = q_ref[0, tile * tq:(tile + 1) * tq, cols(hd)]
        zero = jnp.zeros_like(q2)
        qm = jnp.where(low, q2, zero) if hd % 2 == 0 else jnp.where(low, zero, q2)
        return _dot_nt(ck_ref[0, :, cols(hd)], qm)

    def attend(subtract_max):
        queued = {w: scores(w) for w in range(MEMATTN_LOOKAHEAD)}
        outs = {}
        for w, (tile, hd) in enumerate(work):
            if w + MEMATTN_LOOKAHEAD < len(work):
                queued[w + MEMATTN_LOOKAHEAD] = scores(w + MEMATTN_LOOKAHEAD)
            s = queued.pop(w)
            if subtract_max:
                s = s - jnp.max(s, axis=0, keepdims=True)
            acc = _dot(cvt_ref[0, hd], jnp.exp(s).astype(BF16))
            outs[hd % 2] = acc[:HEAD_DIM] / acc[HEAD_DIM:HEAD_DIM + 1]
            if hd % 2 == 1:
                o_ref[0, tile * tq:(tile + 1) * tq, cols(hd)] = (
                    jnp.concatenate([outs[0], outs[1]], axis=0).T.astype(BF16))

    gq, gk = gq_ref[...], gk_ref[...]
    bounded = (HEAD_DIM * SCORE_BOUND_MARGIN) * jnp.max(gq * gq) * jnp.max(gk * gk) <= MOBA_SAFE_LOGIT ** 2
    pl.when(bounded)(lambda: attend(False))
    pl.when(jnp.logical_not(bounded))(lambda: attend(True))


def _mem_attention(q, ck, cv, gq, gk):
    b, s, w = q.shape
    m = ck.shape[1]
    tm = MEMATTN_ROWS
    return pl.pallas_call(
        _memattn_kernel,
        grid=(b, s // tm),
        in_specs=[
            pl.BlockSpec((1, tm, w), lambda bi, i: (bi, i, 0)),
            pl.BlockSpec((1, m, w), lambda bi, i: (bi, 0, 0)),
            pl.BlockSpec((1, MEM_HEADS, VT_ROWS, m), lambda bi, i: (bi, 0, 0, 0)),
            pl.BlockSpec((1, w), lambda bi, i: (0, 0)),
            pl.BlockSpec((1, w), lambda bi, i: (0, 0)),
        ],
        out_specs=pl.BlockSpec((1, tm, w), lambda bi, i: (bi, i, 0)),
        out_shape=jax.ShapeDtypeStruct((b, s, w), BF16),
        compiler_params=pltpu.CompilerParams(dimension_semantics=("arbitrary", "arbitrary"),
                                             vmem_limit_bytes=VMEM_LIMIT),
        name="mem_attention",
    )(q, ck, cv, gq, gk)


def _sigmoid(t):
    return 1.0 / (1.0 + jnp.exp(-t))


def _tail_kernel(x_ref, osb_ref, omo_ref, ome_ref, gmix_ref, gffn_ref, wg_ref, wsb_ref, wmo_ref, wme_ref,
                 wout_ref, wfi_ref, wfd_ref, out_ref):
    d = x_ref.shape[1]
    dff = wfd_ref.shape[0]
    half = x_ref.shape[0] // 2

    def merged(rows):
        x = x_ref[rows, :]
        h = _rms_rows(x, gmix_ref[...]).astype(BF16)
        mix = None
        for n, (o_ref, wu_ref) in enumerate(((osb_ref, wsb_ref), (omo_ref, wmo_ref), (ome_ref, wme_ref))):
            gate = _dot(h, wg_ref[:, n * d:(n + 1) * d])
            term = _sigmoid(gate) * _dot(o_ref[rows, :], wu_ref[...])
            mix = term if mix is None else mix + term
        return x + _dot(mix.astype(BF16), wout_ref[...])

    def swiglu(x1):
        h2 = _rms_rows(x1, gffn_ref[...]).astype(BF16)
        gate = _dot(h2, wfi_ref[:, :dff])
        up = _dot(h2, wfi_ref[:, dff:])
        ff = (gate * _sigmoid(gate) * up).astype(BF16)
        return x1 + _dot(ff, wfd_ref[...])

    halves = [slice(r * half, (r + 1) * half) for r in range(2)]
    x1s = [merged(rows) for rows in halves]
    for rows, x1 in zip(halves, x1s):
        out_ref[rows, :] = swiglu(x1)


def _tail(x2, osb, omo, ome, gmix, gffn, wg, wsb, wmo, wme, wout, wfi, wfd):
    n, d = x2.shape
    tm = TAIL_ROWS
    rows = lambda i: (i, 0)

    def resident(arr):
        return pl.BlockSpec(arr.shape, lambda i: (0, 0), pipeline_mode=pl.Buffered(1))

    return pl.pallas_call(
        _tail_kernel,
        grid=(n // tm,),
        in_specs=[
            pl.BlockSpec((tm, d), rows),
            pl.BlockSpec((tm, SB_W), rows),
            pl.BlockSpec((tm, MOBA_W), rows),
            pl.BlockSpec((tm, MEM_W), rows),
            resident(gmix), resident(gffn), resident(wg), resident(wsb), resident(wmo), resident(wme),
            resident(wout), resident(wfi), resident(wfd),
        ],
        out_specs=pl.BlockSpec((tm, d), rows),
        out_shape=jax.ShapeDtypeStruct((n, d), F32),
        compiler_params=pltpu.CompilerParams(dimension_semantics=("arbitrary",), vmem_limit_bytes=VMEM_LIMIT),
        name="merge_out_ffn",
    )(x2, osb, omo, ome, gmix, gffn, wg, wsb, wmo, wme, wout, wfi, wfd)


def _rope_tables(seq):
    half = HEAD_DIM // 2
    inv_freq = ROPE_THETA ** (-jnp.arange(half, dtype=F32) * 2.0 / HEAD_DIM)
    ang = jnp.arange(seq, dtype=F32)[:, None] * inv_freq[None, :]
    cos, sin = jnp.cos(ang), jnp.sin(ang)
    reps = LANES // HEAD_DIM
    return jnp.tile(jnp.concatenate([cos, cos], axis=1), (1, reps)), jnp.tile(jnp.concatenate([-sin, sin], axis=1), (1, reps))


def _head_mean_matrix(width):
    r = jnp.arange(width) // HEAD_DIM
    return jnp.where(r[:, None] == r[None, :], 1.0 / HEAD_DIM, 0.0).astype(BF16)


def _layer(x, mem, mix_norm_g, mem_norm_g, ffn_norm_g, w_in, w_mem_kv, moba_q_norm_g, moba_k_norm_g,
           mem_q_norm_g, mem_k_norm_g, w_up_sb, w_up_moba, w_up_mem, w_out, w_ffn_in, w_ffn_down):
    b, s, d = x.shape
    x2 = x.reshape(b * s, d)
    cos, sin = _rope_tables(s)
    bd = _head_mean_matrix(MXU_TILE)
    row = lambda g, reps=1: jnp.tile(g.astype(F32), reps)[None, :]
    gqk = jnp.concatenate([row(moba_q_norm_g, MOBA_HEADS), row(moba_k_norm_g, MOBA_HEADS)], axis=1)

    ck, cv = _memkv(mem, row(mem_norm_g), w_mem_kv.astype(BF16), bd, row(mem_k_norm_g, MEM_HEADS))
    sbq, sbk, sbv, moqt, mok, movt, meq, kmean = _inproj(
        x2, row(mix_norm_g), _split_projection(w_in[:, :QKV_W].astype(BF16)), cos, sin, bd, gqk,
        row(mem_q_norm_g, MEM_HEADS), s)
    seq3 = lambda t: t.reshape(b, s, t.shape[-1])
    o_sb = _sb_attention(seq3(sbq), seq3(sbk), seq3(sbv))
    o_mo = _moba_attention(moqt, seq3(mok), movt, kmean.reshape(b, s // MOBA_BLOCK, MOBA_W), gqk)
    o_me = _mem_attention(seq3(meq), ck, cv, row(mem_q_norm_g, MEM_HEADS), row(mem_k_norm_g, MEM_HEADS))
    flat = lambda t: t.reshape(b * s, t.shape[-1])
    out = _tail(x2, flat(o_sb), flat(o_mo), flat(o_me), row(mix_norm_g), row(ffn_norm_g),
                w_in[:, QKV_W:].astype(BF16), w_up_sb.astype(BF16), w_up_moba.astype(BF16), w_up_mem.astype(BF16),
                w_out.astype(BF16), w_ffn_in.astype(BF16), w_ffn_down.astype(BF16))
    return out.reshape(b, s, d)


def kernel(x, mem, mix_norm_g, mem_norm_g, ffn_norm_g, w_in, w_mem_kv, moba_q_norm_g, moba_k_norm_g, mem_q_norm_g, mem_k_norm_g, w_up_sb, w_up_moba, w_up_mem, w_out, w_ffn_in, w_ffn_down):
    for l in range(w_in.shape[0]):
        x = _layer(x, mem, mix_norm_g[l], mem_norm_g[l], ffn_norm_g[l], w_in[l], w_mem_kv[l],
                   moba_q_norm_g[l], moba_k_norm_g[l], mem_q_norm_g[l], mem_k_norm_g[l],
                   w_up_sb[l], w_up_moba[l], w_up_mem[l], w_out[l], w_ffn_in[l], w_ffn_down[l])
    return x
```
